```python
import jax, jax.numpy as jnp
from jax import lax
import numpy as np

D_MODEL = 1024
BATCH = 32
SEQ = 2048
DEPTH = 1
DEC_BATCH = 8
DEC_SEQ = 64
PAST_LEN = 1024

CHUNK = 64
HEAD_DIM = 64
N_Q_HEADS = 8
N_KV_HEADS = 2
Q_PER_KV = N_Q_HEADS // N_KV_HEADS
ATT_WIDTH = N_Q_HEADS * HEAD_DIM
KV_WIDTH = N_KV_HEADS * HEAD_DIM
WINDOW = 128
WIN_CHUNKS = WINDOW // CHUNK
ROPE_THETA = 10000.0
HG_HEADS = 4
HG_DK = 128
HG_DV = 128
HG_WIDTH = HG_HEADS * HG_DK
HG_BLOCK = 16
N_EXPERTS = 32
TOP_K = 4
D_FF = 1024
SWIGLU_LIMIT = 7.0
SWIGLU_ALPHA = 1.702
MOE_BLOCK = 256
RMS_EPS = 1e-6
N_IN = ATT_WIDTH + 2 * KV_WIDTH + 4 * HG_WIDTH + 2 * D_MODEL

kernel_name = 'hybrid_swa_hgrn2_moe_stream_step'


def rms_norm(x, g):
    xf = x.astype(jnp.float32)
    y = xf * lax.rsqrt(jnp.mean(xf * xf, axis=-1, keepdims=True) + RMS_EPS)
    return (y * g.astype(jnp.float32)).astype(x.dtype)


def rope(x, pos):
    half = HEAD_DIM // 2
    inv_freq = ROPE_THETA ** (-jnp.arange(half, dtype=jnp.float32) / half)
    ang = pos.astype(jnp.float32)[:, None] * inv_freq[None, :]
    cos = jnp.cos(ang)[None, :, None, :]
    sin = jnp.sin(ang)[None, :, None, :]
    xf = x.astype(jnp.float32)
    x1, x2 = xf[..., :half], xf[..., half:]
    return jnp.concatenate([x1 * cos - x2 * sin, x2 * cos + x1 * sin], axis=-1).astype(x.dtype)


def sink_attention(q, k, v, valid, sinks):
    s = jnp.einsum('bnqhgd,bnkhd->bnhgqk', q, k).astype(jnp.float32) * (HEAD_DIM ** -0.5)
    s = jnp.where(valid[None, :, None, None, None, :], s, -jnp.inf)
    sink = sinks.astype(jnp.float32).reshape(1, 1, N_KV_HEADS, Q_PER_KV, 1, 1)
    m = jnp.maximum(jnp.max(s, axis=-1, keepdims=True), sink)
    p = jnp.exp(s - m)
    p = p / (jnp.sum(p, axis=-1, keepdims=True) + jnp.exp(sink - m))
    return jnp.einsum('bnhgqk,bnkhd->bnqhgd', p.astype(v.dtype), v)


def banded_window_attention(q, k, v, sinks, rows):
    B, T = q.shape[:2]
    n_c = T // CHUNK
    pad = WIN_CHUNKS * CHUNK
    kp = jnp.pad(k, ((0, 0), (pad, 0), (0, 0), (0, 0))).reshape(B, n_c + WIN_CHUNKS, CHUNK, N_KV_HEADS, HEAD_DIM)
    vp = jnp.pad(v, ((0, 0), (pad, 0), (0, 0), (0, 0))).reshape(B, n_c + WIN_CHUNKS, CHUNK, N_KV_HEADS, HEAD_DIM)
    k_band = jnp.concatenate([kp[:, j:j + n_c] for j in range(WIN_CHUNKS + 1)], axis=2)
    v_band = jnp.concatenate([vp[:, j:j + n_c] for j in range(WIN_CHUNKS + 1)], axis=2)
    key_pos = jnp.arange(n_c)[:, None] * CHUNK + jnp.arange((WIN_CHUNKS + 1) * CHUNK)[None, :] - pad
    qg = q.reshape(B, n_c, CHUNK, N_KV_HEADS, Q_PER_KV, HEAD_DIM)
    o = sink_attention(qg, k_band, v_band, key_pos >= 0, sinks)
    return o.reshape(B, T, ATT_WIDTH), k[:, T - rows:], v[:, T - rows:]


def cached_window_attention(q, k, v, k_past, v_past, sinks):
    B, T = q.shape[:2]
    rows = k_past.shape[1]
    kb = jnp.concatenate([k_past.astype(k.dtype), k], axis=1)
    vb = jnp.concatenate([v_past.astype(v.dtype), v], axis=1)
    qg = q.reshape(B, 1, T, N_KV_HEADS, Q_PER_KV, HEAD_DIM)
    valid = jnp.ones((1, kb.shape[1]), dtype=bool)
    o = sink_attention(qg, kb[:, None], vb[:, None], valid, sinks)
    n = kb.shape[1]
    return o.reshape(B, T, ATT_WIDTH), kb[:, n - rows:], vb[:, n - rows:]


def hgrn2_chunkwise(q, k, v, log_f, s0):
    B, T, H, _ = q.shape
    pad = (-T) % HG_BLOCK
    n = (T + pad) // HG_BLOCK

    def blocks(a):
        a = jnp.pad(a.astype(jnp.float32), ((0, 0), (0, pad), (0, 0), (0, 0)))
        return a.reshape(B, n, HG_BLOCK, H, a.shape[-1]).transpose(1, 0, 3, 2, 4)

    causal = jnp.tril(jnp.ones((HG_BLOCK, HG_BLOCK), dtype=bool))

    def step(S, blk):
        qb, kb, vb, gb = blk
        b = jnp.cumsum(gb, axis=2)
        qt = qb * jnp.exp(b)
        kt = kb * jnp.exp(-b)
        a = jnp.where(causal, jnp.einsum('bhtk,bhsk->bhts', qt, kt), 0.0)
        o = jnp.einsum('bhts,bhsv->bhtv', a, vb) + jnp.einsum('bhtk,bhkv->bhtv', qt, S)
        b_end = b[:, :, -1:, :]
        S = jnp.exp(b_end[:, :, 0, :, None]) * S + jnp.einsum('bhsk,bhsv->bhkv', kb * jnp.exp(b_end - b), vb)
        return S, o

    s_final, o = lax.scan(step, s0.astype(jnp.float32), (blocks(q), blocks(k), blocks(v), blocks(log_f)))
    o = o.transpose(1, 0, 3, 2, 4).reshape(B, n * HG_BLOCK, H, HG_DV)[:, :T]
    return o.astype(q.dtype), s_final.astype(s0.dtype)


def mixer(h, pos, k_past, v_past, s0, rows, lb, w_in, sinks, g_hgrn, w_br_attn, w_br_hgrn, w_out):
    B, T, _ = h.shape
    sizes = [ATT_WIDTH, KV_WIDTH, KV_WIDTH, HG_WIDTH, HG_WIDTH, HG_WIDTH, HG_WIDTH, D_MODEL, D_MODEL]
    cuts = [int(c) for c in np.cumsum(sizes)[:-1]]
    q, k, v, hq, hf, hi, hz, za, zh = jnp.split(h @ w_in, cuts, axis=-1)
    q = rope(q.reshape(B, T, N_Q_HEADS, HEAD_DIM), pos)
    k = rope(k.reshape(B, T, N_KV_HEADS, HEAD_DIM), pos)
    v = v.reshape(B, T, N_KV_HEADS, HEAD_DIM)
    if k_past is None:
        y_att, k_new, v_new = banded_window_attention(q, k, v, sinks, rows)
    else:
        y_att, k_new, v_new = cached_window_attention(q, k, v, k_past, v_past, sinks)
    f = lb + (1.0 - lb) * jax.nn.sigmoid(hf.astype(jnp.float32))
    shp = (B, T, HG_HEADS, HG_DK)
    o, s_new = hgrn2_chunkwise(jax.nn.silu(hq).reshape(shp), (1.0 - f).reshape(shp),
                               hi.reshape(B, T, HG_HEADS, HG_DV), jnp.log(f).reshape(shp), s0)
    o = rms_norm(o, g_hgrn.reshape(HG_HEADS, HG_DV)) * jax.nn.silu(hz.reshape(B, T, HG_HEADS, HG_DV))
    y_hg = o.reshape(B, T, HG_WIDTH)
    merged = jax.nn.sigmoid(za) * (y_att @ w_br_attn) + jax.nn.sigmoid(zh) * (y_hg @ w_br_hgrn)
    return merged @ w_out, k_new, v_new, s_new


def moe(h, w_router, b_router, w_gate_up, b_gate_up, w_down, b_down):
    B, T, D = h.shape
    xt = h.reshape(B * T, D)
    n_tok = B * T
    logits = (xt @ w_router + b_router).astype(jnp.float32)
    top_val, top_idx = lax.top_k(logits, TOP_K)
    gates = jax.nn.softmax(top_val, axis=-1)
    n_assign = n_tok * TOP_K
    e_flat = top_idx.reshape(-1)
    order = jnp.argsort(e_flat)
    e_sorted = e_flat[order]
    tok_sorted = order // TOP_K
    gate_sorted = gates.reshape(-1)[order]
    counts = jnp.bincount(e_flat, length=N_EXPERTS)
    padded = (counts + MOE_BLOCK - 1) // MOE_BLOCK * MOE_BLOCK
    start = jnp.cumsum(counts) - counts
    pend = jnp.cumsum(padded)
    pstart = pend - padded
    dest = pstart[e_sorted] + jnp.arange(n_assign) - start[e_sorted]
    n_blocks = -(-(n_assign + N_EXPERTS * (MOE_BLOCK - 1)) // MOE_BLOCK)
    buf = jnp.zeros((n_blocks * MOE_BLOCK, D), xt.dtype).at[dest].set(xt[tok_sorted])
    blk_expert = jnp.minimum(jnp.searchsorted(pend, jnp.arange(n_blocks) * MOE_BLOCK, side='right'), N_EXPERTS - 1)

    def expert_block(args):
        xb, e = args
        gu = xb @ w_gate_up[e] + b_gate_up[e]
        gate = jnp.minimum(gu[:, :D_FF], SWIGLU_LIMIT)
        up = jnp.clip(gu[:, D_FF:], -SWIGLU_LIMIT, SWIGLU_LIMIT)
        act = gate * jax.nn.sigmoid(SWIGLU_ALPHA * gate) * (up + 1.0)
        return act @ w_down[e] + b_down[e]

    y_buf = lax.map(expert_block, (buf.reshape(n_blocks, MOE_BLOCK, D), blk_expert))
    y = y_buf.reshape(n_blocks * MOE_BLOCK, D)[dest].astype(jnp.float32)
    out = jnp.zeros((n_tok, D), jnp.float32).at[tok_sorted].add(gate_sorted[:, None] * y)
    return out.astype(h.dtype).reshape(B, T, D)


def trunk_layer(x, c, pos, k_past, v_past, s0, rows, lb, w_ada, b_ada, g_pre_mix, g_post_mix,
                g_pre_ffn, g_post_ffn, w_in, sinks, g_hgrn, w_br_attn, w_br_hgrn, w_out,
                w_router, b_router, w_gate_up, b_gate_up, w_down, b_down):
    ada = (jax.nn.silu(c) @ w_ada + b_ada)[:, None, :]
    shift1, scale1, gate1, shift2, scale2, gate2 = jnp.split(ada, 6, axis=-1)
    h = rms_norm(x, g_pre_mix) * (1.0 + scale1) + shift1
    y, k_new, v_new, s_new = mixer(h, pos, k_past, v_past, s0, rows, lb, w_in, sinks, g_hgrn,
                                   w_br_attn, w_br_hgrn, w_out)
    x = x + gate1 * rms_norm(y, g_post_mix)
    h = rms_norm(x, g_pre_ffn) * (1.0 + scale2) + shift2
    x = x + gate2 * rms_norm(moe(h, w_router, b_router, w_gate_up, b_gate_up, w_down, b_down), g_post_ffn)
    return x, k_new, v_new, s_new


def setup_inputs(seed: int = 0) -> dict:
    key = jax.random.key(seed)
    ks = jax.random.split(key, 26)
    rows = min(WINDOW, PAST_LEN)

    def nrm(k, shape, scale):
        return jax.random.normal(k, shape, jnp.float32) * scale

    return {
        'x_prompt': nrm(ks[0], (BATCH, SEQ, D_MODEL), 1.0),
        'x_sample': nrm(ks[1], (DEC_BATCH, DEC_SEQ, D_MODEL), 1.0),
        'cache_k': nrm(ks[2], (DEPTH, DEC_BATCH, rows, N_KV_HEADS, HEAD_DIM), 1.0),
        'cache_v': nrm(ks[3], (DEPTH, DEC_BATCH, rows, N_KV_HEADS, HEAD_DIM), 1.0),
        'state_hgrn': nrm(ks[4], (DEPTH, DEC_BATCH, HG_HEADS, HG_DK, HG_DV), 0.5),
        'c_prompt': nrm(ks[5], (BATCH, D_MODEL), 1.0),
        'c_sample': nrm(ks[6], (DEC_BATCH, D_MODEL), 1.0),
        'w_ada': nrm(ks[7], (DEPTH, D_MODEL, 6 * D_MODEL), 0.5 * D_MODEL ** -0.5),
        'b_ada': nrm(ks[8], (DEPTH, 6 * D_MODEL), 0.02),
        'g_pre_mix': 1.0 + nrm(ks[9], (DEPTH, D_MODEL), 0.05),
        'g_post_mix': 1.0 + nrm(ks[10], (DEPTH, D_MODEL), 0.05),
        'g_pre_ffn': 1.0 + nrm(ks[11], (DEPTH, D_MODEL), 0.05),
        'g_post_ffn': 1.0 + nrm(ks[12], (DEPTH, D_MODEL), 0.05),
        'w_in': nrm(ks[13], (DEPTH, D_MODEL, N_IN), D_MODEL ** -0.5),
        'attn_sinks': nrm(ks[14], (DEPTH, N_Q_HEADS), 1.0),
        'hgrn_lb': nrm(ks[15], (DEPTH + 1, HG_WIDTH), 0.5),
        'g_hgrn': 1.0 + nrm(ks[16], (DEPTH, HG_WIDTH), 0.05),
        'w_br_attn': nrm(ks[17], (DEPTH, ATT_WIDTH, D_MODEL), ATT_WIDTH ** -0.5),
        'w_br_hgrn': nrm(ks[18], (DEPTH, HG_WIDTH, D_MODEL), HG_WIDTH ** -0.5),
        'w_out': nrm(ks[19], (DEPTH, D_MODEL, D_MODEL), D_MODEL ** -0.5),
        'w_router': nrm(ks[20], (DEPTH, D_MODEL, N_EXPERTS), D_MODEL ** -0.5),
        'b_router': nrm(ks[21], (DEPTH, N_EXPERTS), 0.01),
        'w_gate_up': nrm(ks[22], (DEPTH, N_EXPERTS, D_MODEL, 2 * D_FF), D_MODEL ** -0.5),
        'b_gate_up': nrm(ks[23], (DEPTH, N_EXPERTS, 2 * D_FF), 0.02),
        'w_down': nrm(ks[24], (DEPTH, N_EXPERTS, D_FF, D_MODEL), D_FF ** -0.5),
        'b_down': nrm(ks[25], (DEPTH, N_EXPERTS, D_MODEL), 0.02),
    }


def reference(x_prompt, x_sample, cache_k, cache_v, state_hgrn, c_prompt, c_sample, w_ada, b_ada,
              g_pre_mix, g_post_mix, g_pre_ffn, g_post_ffn, w_in, attn_sinks, hgrn_lb, g_hgrn,
              w_br_attn, w_br_hgrn, w_out, w_router, b_router, w_gate_up, b_gate_up, w_down, b_down):
    rows = cache_k.shape[2]
    lb_layers = jnp.cumsum(jax.nn.softmax(hgrn_lb.astype(jnp.float32), axis=0), axis=0)
    pos_p = jnp.arange(x_prompt.shape[1])
    pos_s = PAST_LEN + jnp.arange(x_sample.shape[1])
    s0_prompt = jnp.zeros((x_prompt.shape[0], HG_HEADS, HG_DK, HG_DV), x_prompt.dtype)
    xp, xs = x_prompt, x_sample
    kp_l, vp_l, sp_l, ks_l, vs_l, ss_l = [], [], [], [], [], []
    for l in range(DEPTH):
        xp, kp, vp, sp = trunk_layer(
            xp, c_prompt, pos_p, None, None, s0_prompt, rows, lb_layers[l], w_ada[l], b_ada[l],
            g_pre_mix[l], g_post_mix[l], g_pre_ffn[l], g_post_ffn[l], w_in[l], attn_sinks[l], g_hgrn[l],
            w_br_attn[l], w_br_hgrn[l], w_out[l], w_router[l], b_router[l], w_gate_up[l], b_gate_up[l],
            w_down[l], b_down[l])
        xs, ks_, vs_, ss = trunk_layer(
            xs, c_sample, pos_s, cache_k[l], cache_v[l], state_hgrn[l], rows, lb_layers[l], w_ada[l], b_ada[l],
            g_pre_mix[l], g_post_mix[l], g_pre_ffn[l], g_post_ffn[l], w_in[l], attn_sinks[l], g_hgrn[l],
            w_br_attn[l], w_br_hgrn[l], w_out[l], w_router[l], b_router[l], w_gate_up[l], b_gate_up[l],
            w_down[l], b_down[l])
        kp_l.append(kp); vp_l.append(vp); sp_l.append(sp)
        ks_l.append(ks_); vs_l.append(vs_); ss_l.append(ss)
    return (xp, xs, jnp.stack(kp_l), jnp.stack(vp_l), jnp.stack(sp_l),
            jnp.stack(ks_l), jnp.stack(vs_l), jnp.stack(ss_l))
```

```python
import functools

import jax
import jax.numpy as jnp
from jax import lax
from jax.experimental import pallas as pl
from jax.experimental.pallas import tpu as pltpu

F32 = jnp.float32
BF16 = jnp.bfloat16
I32 = jnp.int32

D_MODEL = 1024
PAST_LEN = 1024
CHUNK = 64
HEAD_DIM = 64
N_Q_HEADS = 8
N_KV_HEADS = 2
Q_PER_KV = N_Q_HEADS // N_KV_HEADS
ATT_WIDTH = N_Q_HEADS * HEAD_DIM
KV_WIDTH = N_KV_HEADS * HEAD_DIM
WINDOW = 128
ROPE_THETA = 10000.0
HG_HEADS = 4
HG_DK = 128
HG_DV = 128
HG_WIDTH = HG_HEADS * HG_DK
HG_BLOCK = 16
N_EXPERTS = 32
TOP_K = 4
D_FF = 1024
SWIGLU_LIMIT = 7.0
SWIGLU_ALPHA = 1.702
RMS_EPS = 1e-6
N_IN = ATT_WIDTH + 2 * KV_WIDTH + 4 * HG_WIDTH + 2 * D_MODEL
HG_OFF = ATT_WIDTH + 2 * KV_WIDTH
Z_OFF = HG_OFF + 4 * HG_WIDTH

LANES = 128
VMEM_LIMIT = 56 * 1024 * 1024

_NT = (((1,), (1,)), ((), ()))
_TN = (((0,), (0,)), ((), ()))


def _sigmoid(x):
    return 1.0 / (1.0 + jnp.exp(-x))


def _rms(x):
    return x * lax.rsqrt(jnp.mean(x * x, axis=-1, keepdims=True) + RMS_EPS)


def _cparams(n_axes):
    return pltpu.CompilerParams(dimension_semantics=("arbitrary",) * n_axes,
                                vmem_limit_bytes=VMEM_LIMIT)


def _ada_kernel(c_ref, w_ref, b_ref, o_ref):
    c = c_ref[...]
    s = c * _sigmoid(c)
    o_ref[...] = jnp.dot(s, w_ref[...], precision=lax.Precision.HIGHEST,
                         preferred_element_type=F32) + b_ref[...]


def _ada(c, w_ada, b_ada):
    nb = c.shape[0]
    n_out = w_ada.shape[1]
    tn = D_MODEL
    return pl.pallas_call(
        _ada_kernel,
        grid=(n_out // tn,),
        in_specs=[pl.BlockSpec((nb, D_MODEL), lambda j: (0, 0)),
                  pl.BlockSpec((D_MODEL, tn), lambda j: (0, j)),
                  pl.BlockSpec((1, tn), lambda j: (0, j))],
        out_specs=pl.BlockSpec((nb, tn), lambda j: (0, j)),
        out_shape=jax.ShapeDtypeStruct((nb, n_out), F32),
        compiler_params=_cparams(1),
        name="ada",
    )(c, w_ada, b_ada.reshape(1, n_out))


def _proj_kernel(x_ref, mod_ref, g_ref, w_ref, cos_ref, sin_ref,
                 q_ref, k_ref, v_ref, hg_ref, z_ref):
    x = x_ref[...]
    h = _rms(x) * g_ref[...]
    h = h * (1.0 + mod_ref[1:2, :]) + mod_ref[0:1, :]
    hb = h.astype(BF16)
    cos = cos_ref[...]
    sin = sin_ref[...]
    lane = lax.broadcasted_iota(I32, cos.shape, 1)
    first_half = (lane % HEAD_DIM) < (HEAD_DIM // 2)

    def cols(start, width):
        return jnp.dot(hb, w_ref[:, start:start + width], preferred_element_type=F32)

    def rope(p):
        partner = jnp.where(first_half, pltpu.roll(p, LANES - HEAD_DIM // 2, 1),
                            pltpu.roll(p, HEAD_DIM // 2, 1))
        return p * cos + partner * sin

    for j in range(ATT_WIDTH // LANES):
        q = rope(cols(j * LANES, LANES)) * (HEAD_DIM ** -0.5)
        q_ref[:, j * LANES:(j + 1) * LANES] = q.astype(BF16)
    k_ref[...] = rope(cols(ATT_WIDTH, KV_WIDTH))
    v_ref[...] = cols(ATT_WIDTH + KV_WIDTH, KV_WIDTH)
    for j in range(4):
        hg_ref[:, j * HG_WIDTH:(j + 1) * HG_WIDTH] = cols(HG_OFF + j * HG_WIDTH, HG_WIDTH)
    for j in range(4):
        z_ref[:, j * 512:(j + 1) * 512] = cols(Z_OFF + j * 512, 512)


def _proj(x2, ada3, b_off, seq, g_pre, w_in_bf, cos_t, sin_t, tm):
    n = x2.shape[0]
    tpb = seq // tm
    row = lambda i: (i, 0)
    return pl.pallas_call(
        _proj_kernel,
        grid=(n // tm,),
        in_specs=[pl.BlockSpec((tm, D_MODEL), row),
                  pl.BlockSpec((None, 6, D_MODEL), lambda i: (i // tpb + b_off, 0, 0)),
                  pl.BlockSpec((1, D_MODEL), lambda i: (0, 0)),
                  pl.BlockSpec((D_MODEL, N_IN), lambda i: (0, 0)),
                  pl.BlockSpec((tm, LANES), lambda i: (i % tpb, 0)),
                  pl.BlockSpec((tm, LANES), lambda i: (i % tpb, 0))],
        out_specs=[pl.BlockSpec((tm, ATT_WIDTH), row),
                   pl.BlockSpec((tm, KV_WIDTH), row),
                   pl.BlockSpec((tm, KV_WIDTH), row),
                   pl.BlockSpec((tm, 4 * HG_WIDTH), row),
                   pl.BlockSpec((tm, 2 * D_MODEL), row)],
        out_shape=[jax.ShapeDtypeStruct((n, ATT_WIDTH), BF16),
                   jax.ShapeDtypeStruct((n, KV_WIDTH), F32),
                   jax.ShapeDtypeStruct((n, KV_WIDTH), F32),
                   jax.ShapeDtypeStruct((n, 4 * HG_WIDTH), F32),
                   jax.ShapeDtypeStruct((n, 2 * D_MODEL), F32)],
        compiler_params=_cparams(1),
        name="proj",
    )(x2, ada3, g_pre.reshape(1, D_MODEL), w_in_bf, cos_t, sin_t)


def _attn_kernel(sink_ref, q_ref, k_ref, v_ref, o_ref, *, tq, koff):
    t = pl.program_id(1)
    span = WINDOW + CHUNK
    for c in range(tq // CHUNK):
        r = t * tq + c * CHUNK + koff
        ks = pl.multiple_of(jnp.maximum(r - WINDOW, 0), CHUNK)
        kb = k_ref[pl.ds(ks, span), :].astype(BF16)
        vb = v_ref[pl.ds(ks, span), :].astype(BF16)
        kpos = ks + lax.broadcasted_iota(I32, (1, span), 1)
        valid = kpos < r + CHUNK
        qc = q_ref[c * CHUNK:(c + 1) * CHUNK, :]
        outs = []
        for h in range(N_Q_HEADS):
            g = h // Q_PER_KV
            qh = qc[:, h * HEAD_DIM:(h + 1) * HEAD_DIM]
            kg = kb[:, g * HEAD_DIM:(g + 1) * HEAD_DIM]
            vg = vb[:, g * HEAD_DIM:(g + 1) * HEAD_DIM]
            s = lax.dot_general(qh, kg, _NT, preferred_element_type=F32)
            s = jnp.where(valid, s, -jnp.inf)
            sink = sink_ref[h]
            m = jnp.maximum(jnp.max(s, axis=-1, keepdims=True), sink)
            p = jnp.exp(s - m)
            p = p / (jnp.sum(p, axis=-1, keepdims=True) + jnp.exp(sink - m))
            outs.append(jnp.dot(p.astype(BF16), vg, preferred_element_type=F32))
        o_ref[c * CHUNK:(c + 1) * CHUNK, :] = jnp.concatenate(outs, axis=1).astype(BF16)


def _attn(q3, k3, v3, sinks, tq, koff):
    nb, tl, _ = q3.shape
    tk = k3.shape[1]
    return pl.pallas_call(
        functools.partial(_attn_kernel, tq=tq, koff=koff),
        grid=(nb, tl // tq),
        in_specs=[pl.BlockSpec(memory_space=pltpu.SMEM),
                  pl.BlockSpec((None, tq, ATT_WIDTH), lambda b, t: (b, t, 0)),
                  pl.BlockSpec((None, tk, KV_WIDTH), lambda b, t: (b, 0, 0)),
                  pl.BlockSpec((None, tk, KV_WIDTH), lambda b, t: (b, 0, 0))],
        out_specs=pl.BlockSpec((None, tq, ATT_WIDTH), lambda b, t: (b, t, 0)),
        out_shape=jax.ShapeDtypeStruct((nb, tl, ATT_WIDTH), BF16),
        compiler_params=_cparams(2),
        name="attn",
    )(sinks, q3, k3, v3)


def _hgrn_kernel(*refs, tc, has_s0):
    if has_s0:
        hg_ref, lbp_ref, gh_ref, s0_ref, y_ref, s_out_ref, st_ref = refs
    else:
        hg_ref, lbp_ref, gh_ref, y_ref, s_out_ref, st_ref = refs
        s0_ref = None
    t = pl.program_id(1)

    @pl.when(t == 0)
    def _():
        for h in range(HG_HEADS):
            st_ref[h] = s0_ref[h].T if has_s0 else jnp.zeros((HG_DV, HG_DK), F32)

    l0 = lbp_ref[0:1, :]
    l1 = lbp_ref[1:2, :]
    lm = jnp.maximum(l0, l1)
    e0 = jnp.exp(l0 - lm)
    lb = e0 / (e0 + jnp.exp(l1 - lm))

    hq = hg_ref[:, 0:HG_WIDTH]
    hf = hg_ref[:, HG_WIDTH:2 * HG_WIDTH]
    hi = hg_ref[:, 2 * HG_WIDTH:3 * HG_WIDTH]
    hz = hg_ref[:, 3 * HG_WIDTH:4 * HG_WIDTH]
    f = lb + (1.0 - lb) * _sigmoid(hf)
    q = hq * _sigmoid(hq)
    kk = 1.0 - f
    g = jnp.log(f)

    r16 = lax.broadcasted_iota(I32, g.shape, 0) % HG_BLOCK
    b = g
    suf = g
    for s in (1, 2, 4, 8):
        b = b + jnp.where(r16 >= s, pltpu.roll(b, s, 0), 0.0)
        suf = suf + jnp.where(r16 < HG_BLOCK - s, pltpu.roll(suf, tc - s, 0), 0.0)
    qt = (q * jnp.exp(b)).astype(BF16)
    kt = (kk * jnp.exp(-b)).astype(BF16)
    kd = (kk * jnp.exp(suf - g)).astype(BF16)
    vb = hi.astype(BF16)
    causal = (lax.broadcasted_iota(I32, (HG_BLOCK, HG_BLOCK), 0)
              >= lax.broadcasted_iota(I32, (HG_BLOCK, HG_BLOCK), 1))

    o_heads = []
    for h in range(HG_HEADS):
        cs = slice(h * HG_DK, (h + 1) * HG_DK)
        st = st_ref[h]
        o_blocks = []
        for j in range(tc // HG_BLOCK):
            rs = slice(j * HG_BLOCK, (j + 1) * HG_BLOCK)
            qt_b, kt_b, kd_b, v_b = qt[rs, cs], kt[rs, cs], kd[rs, cs], vb[rs, cs]
            a = lax.dot_general(qt_b, kt_b, _NT, preferred_element_type=F32)
            a = jnp.where(causal, a, 0.0)
            o = (jnp.dot(a.astype(BF16), v_b, preferred_element_type=F32)
                 + lax.dot_general(qt_b, st.astype(BF16), _NT, preferred_element_type=F32))
            o_blocks.append(o)
            dec = jnp.exp(b[j * HG_BLOCK + HG_BLOCK - 1:(j + 1) * HG_BLOCK, cs])
            st = st * dec + lax.dot_general(v_b, kd_b, _TN, preferred_element_type=F32)
        st_ref[h] = st
        o_h = jnp.concatenate(o_blocks, axis=0)
        zs = hz[:, cs]
        o_heads.append(_rms(o_h) * gh_ref[:, cs] * (zs * _sigmoid(zs)))
    y_ref[...] = jnp.concatenate(o_heads, axis=1).astype(BF16)

    @pl.when(t == pl.num_programs(1) - 1)
    def _():
        for h in range(HG_HEADS):
            s_out_ref[h] = st_ref[h].T


def _hgrn(hg3, hgrn_lb, g_hgrn, s0, tc):
    nb, tl, _ = hg3.shape
    has_s0 = s0 is not None
    st_spec = pl.BlockSpec((None, HG_HEADS, HG_DK, HG_DV), lambda b, t: (b, 0, 0, 0))
    in_specs = [pl.BlockSpec((None, tc, 4 * HG_WIDTH), lambda b, t: (b, t, 0)),
                pl.BlockSpec((2, HG_WIDTH), lambda b, t: (0, 0)),
                pl.BlockSpec((1, HG_WIDTH), lambda b, t: (0, 0))]
    args = [hg3, hgrn_lb, g_hgrn.reshape(1, HG_WIDTH)]
    if has_s0:
        in_specs.append(st_spec)
        args.append(s0)
    return pl.pallas_call(
        functools.partial(_hgrn_kernel, tc=tc, has_s0=has_s0),
        grid=(nb, tl // tc),
        in_specs=in_specs,
        out_specs=[pl.BlockSpec((None, tc, HG_WIDTH), lambda b, t: (b, t, 0)), st_spec],
        out_shape=[jax.ShapeDtypeStruct((nb, tl, HG_WIDTH), BF16),
                   jax.ShapeDtypeStruct((nb, HG_HEADS, HG_DK, HG_DV), F32)],
        scratch_shapes=[pltpu.VMEM((HG_HEADS, HG_DV, HG_DK), F32)],
        compiler_params=_cparams(2),
        name="hgrn",
    )(*args)


def _merge_kernel(a_ref, b_ref, z_ref, x_ref, mod_ref, wa_ref, wb_ref, wo_ref, gpost_ref, gpre_ref,
                  wr_ref, br_ref, x1_ref, h2_ref, idx_ref, gate_ref, rank_ref, cnt_ref, carry_ref, *, tm, nseg):
    i = pl.program_id(0)

    def mod_rows(r):
        if nseg == 1:
            return mod_ref[0, r:r + 1, :]
        seg = tm // nseg
        return jnp.concatenate([jnp.broadcast_to(mod_ref[s, r:r + 1, :], (seg, D_MODEL)) for s in range(nseg)],
                               axis=0)

    @pl.when(i == 0)
    def _():
        carry_ref[...] = jnp.zeros_like(carry_ref)

    za = z_ref[:, 0:D_MODEL]
    zh = z_ref[:, D_MODEL:2 * D_MODEL]
    m = (_sigmoid(za) * jnp.dot(a_ref[...], wa_ref[...], preferred_element_type=F32)
         + _sigmoid(zh) * jnp.dot(b_ref[...], wb_ref[...], preferred_element_type=F32))
    y = jnp.dot(m.astype(BF16), wo_ref[...], preferred_element_type=F32)
    x1 = x_ref[...] + mod_rows(2) * (_rms(y) * gpost_ref[...])
    x1_ref[...] = x1
    h2 = (_rms(x1) * gpre_ref[...]) * (1.0 + mod_rows(4)) + mod_rows(3)
    h2_ref[...] = h2

    logits = lax.dot_general(wr_ref[...], h2, _NT, precision=lax.Precision.HIGHEST,
                             preferred_element_type=F32) + br_ref[...]
    eio = lax.broadcasted_iota(I32, logits.shape, 0).astype(F32)
    vals, sels = [], []
    cur = logits
    for k in range(TOP_K):
        mx = jnp.max(cur, axis=0, keepdims=True)
        ik = jnp.min(jnp.where(cur == mx, eio, float(N_EXPERTS)), axis=0, keepdims=True)
        sel = eio == ik
        idx_ref[k:k + 1, :] = ik.astype(I32)
        vals.append(mx)
        sels.append(sel)
        cur = jnp.where(sel, -jnp.inf, cur)
    es = [jnp.exp(v - vals[0]) for v in vals]
    den = es[0] + es[1] + es[2] + es[3]
    for k in range(TOP_K):
        gate_ref[k:k + 1, :] = es[k] / den

    onehot = jnp.zeros(logits.shape, F32)
    for k in range(TOP_K):
        onehot = onehot + sels[k].astype(F32)
    earlier = (lax.broadcasted_iota(I32, (tm, tm), 0) < lax.broadcasted_iota(I32, (tm, tm), 1))
    prefix = jnp.dot(onehot.astype(BF16), earlier.astype(BF16), preferred_element_type=F32)
    base = prefix + carry_ref[:, 0:1]
    for k in range(TOP_K):
        rank_ref[k:k + 1, :] = jnp.sum(jnp.where(sels[k], base, 0.0), axis=0, keepdims=True).astype(I32)
    carry_ref[...] = carry_ref[...] + jnp.sum(onehot, axis=1, keepdims=True)
    cnt_ref[...] = carry_ref[...]


def _merge(a2, b2, z2, x2, ada3, b_off, seq, wa, wb, wo, g_post, g_pre, wr_t, br, tm):
    n = x2.shape[0]
    if tm >= seq:
        nseg = tm // seq
        assert b_off % nseg == 0
        mod_idx = lambda i: (i + b_off // nseg, 0, 0)
    else:
        nseg = 1
        tpb = seq // tm
        mod_idx = lambda i: (i // tpb + b_off, 0, 0)
    row = lambda i: (i, 0)
    const = lambda i: (0, 0)
    col = lambda i: (0, i)
    return pl.pallas_call(
        functools.partial(_merge_kernel, tm=tm, nseg=nseg),
        grid=(n // tm,),
        in_specs=[pl.BlockSpec((tm, ATT_WIDTH), row),
                  pl.BlockSpec((tm, HG_WIDTH), row),
                  pl.BlockSpec((tm, 2 * D_MODEL), row),
                  pl.BlockSpec((tm, D_MODEL), row),
                  pl.BlockSpec((nseg, 6, D_MODEL), mod_idx),
                  pl.BlockSpec((ATT_WIDTH, D_MODEL), const),
                  pl.BlockSpec((HG_WIDTH, D_MODEL), const),
                  pl.BlockSpec((D_MODEL, D_MODEL), const),
                  pl.BlockSpec((1, D_MODEL), const),
                  pl.BlockSpec((1, D_MODEL), const),
                  pl.BlockSpec((N_EXPERTS, D_MODEL), const),
                  pl.BlockSpec((N_EXPERTS, 1), const)],
        out_specs=[pl.BlockSpec((tm, D_MODEL), row),
                   pl.BlockSpec((tm, D_MODEL), row),
                   pl.BlockSpec((TOP_K, tm), col),
                   pl.BlockSpec((TOP_K, tm), col),
                   pl.BlockSpec((TOP_K, tm), col),
                   pl.BlockSpec((N_EXPERTS, LANES), const)],
        out_shape=[jax.ShapeDtypeStruct((n, D_MODEL), F32),
                   jax.ShapeDtypeStruct((n, D_MODEL), F32),
                   jax.ShapeDtypeStruct((TOP_K, n), I32),
                   jax.ShapeDtypeStruct((TOP_K, n), F32),
                   jax.ShapeDtypeStruct((TOP_K, n), I32),
                   jax.ShapeDtypeStruct((N_EXPERTS, LANES), F32)],
        scratch_shapes=[pltpu.VMEM((N_EXPERTS, LANES), F32)],
        compiler_params=_cparams(1),
        name="merge",
    )(a2, b2, z2, x2, ada3, wa, wb, wo, g_post.reshape(1, D_MODEL), g_pre.reshape(1, D_MODEL),
      wr_t, br.reshape(N_EXPERTS, 1))


def _dispatch_kernel(dest_ref, pad_ref, h_ref, buf_ref, zero_ref, sem, zsem, *, td):
    i = pl.program_id(0)

    def row_copy(t, d):
        return pltpu.make_async_copy(h_ref.at[pl.ds(t, 1)], buf_ref.at[pl.ds(d, 1)], sem)

    def zero_copy(d):
        return pltpu.make_async_copy(zero_ref, buf_ref.at[pl.ds(d, 1)], zsem)

    @pl.when(i == 0)
    def _():
        zero_ref[...] = jnp.zeros_like(zero_ref)
        for e in range(N_EXPERTS + 1):
            start = pad_ref[0, e]
            count = pad_ref[1, e]

            def zstart(r, _):
                zero_copy(start + r).start()
                return 0

            def zwait(r, _):
                zero_copy(start + r).wait()
                return 0

            lax.fori_loop(0, count, zstart, 0)
            lax.fori_loop(0, count, zwait, 0)

    def issue(t, _):
        for k in range(TOP_K):
            row_copy(t, dest_ref[0, k * td + t]).start()
        return 0

    def drain(t, _):
        for k in range(TOP_K):
            row_copy(t, dest_ref[0, k * td + t]).wait()
        return 0

    lax.fori_loop(0, td, issue, 0)
    lax.fori_loop(0, td, drain, 0)


def _dispatch(h2, dest_tiles, pad_info, n_rows, td):
    n = h2.shape[0]
    return pl.pallas_call(
        functools.partial(_dispatch_kernel, td=td),
        grid=(n // td,),
        in_specs=[pl.BlockSpec((None, 1, TOP_K * td), lambda i: (i, 0, 0), memory_space=pltpu.SMEM),
                  pl.BlockSpec(memory_space=pltpu.SMEM),
                  pl.BlockSpec((td, D_MODEL), lambda i: (i, 0))],
        out_specs=pl.BlockSpec(memory_space=pl.ANY),
        out_shape=jax.ShapeDtypeStruct((n_rows, D_MODEL), F32),
        scratch_shapes=[pltpu.VMEM((1, D_MODEL), F32), pltpu.SemaphoreType.DMA, pltpu.SemaphoreType.DMA],
        compiler_params=_cparams(1),
        name="dispatch",
    )(dest_tiles, pad_info, h2)


def _expert_kernel(be_ref, x_ref, wgu_ref, bgu_ref, wd_ref, bd_ref, y_ref):
    del be_ref
    gu = jnp.dot(x_ref[...].astype(BF16), wgu_ref[...], preferred_element_type=F32) + bgu_ref[...]
    gate = jnp.minimum(gu[:, :D_FF], SWIGLU_LIMIT)
    up = jnp.clip(gu[:, D_FF:], -SWIGLU_LIMIT, SWIGLU_LIMIT)
    act = gate * _sigmoid(SWIGLU_ALPHA * gate) * (up + 1.0)
    y_ref[...] = jnp.dot(act.astype(BF16), wd_ref[...], preferred_element_type=F32) + bd_ref[...]


def _experts(buf, blk_expert, wgu_bf, bgu, wd_bf, bd, blk):
    n_rows = buf.shape[0]
    grid_spec = pltpu.PrefetchScalarGridSpec(
        num_scalar_prefetch=1,
        grid=(n_rows // blk,),
        in_specs=[pl.BlockSpec((blk, D_MODEL), lambda i, be: (i, 0)),
                  pl.BlockSpec((None, D_MODEL, 2 * D_FF), lambda i, be: (be[i], 0, 0)),
                  pl.BlockSpec((None, 1, 2 * D_FF), lambda i, be: (be[i], 0, 0)),
                  pl.BlockSpec((None, D_FF, D_MODEL), lambda i, be: (be[i], 0, 0)),
                  pl.BlockSpec((None, 1, D_MODEL), lambda i, be: (be[i], 0, 0))],
        out_specs=pl.BlockSpec((blk, D_MODEL), lambda i, be: (i, 0)),
    )
    return pl.pallas_call(
        _expert_kernel,
        grid_spec=grid_spec,
        out_shape=jax.ShapeDtypeStruct((n_rows, D_MODEL), F32),
        compiler_params=_cparams(1),
        name="experts",
    )(blk_expert, buf, wgu_bf, bgu.reshape(N_EXPERTS, 1, 2 * D_FF), wd_bf, bd.reshape(N_EXPERTS, 1, D_MODEL))


def _combine_kernel(dest_ref, gate_ref, x1_ref, mod_ref, gpost_ref, ybuf_ref, o_ref, rows_ref, sem, *, td):
    def row_copy(k, t, d):
        return pltpu.make_async_copy(ybuf_ref.at[pl.ds(d, 1)], rows_ref.at[k, pl.ds(t, 1)], sem)

    def issue(t, _):
        for k in range(TOP_K):
            row_copy(k, t, dest_ref[0, k * td + t]).start()
        return 0

    def drain(t, _):
        for k in range(TOP_K):
            row_copy(k, t, dest_ref[0, k * td + t]).wait()
        return 0

    lax.fori_loop(0, td, issue, 0)
    lax.fori_loop(0, td, drain, 0)
    mo = gate_ref[:, 0:1] * rows_ref[0]
    for k in range(1, TOP_K):
        mo = mo + gate_ref[:, k:k + 1] * rows_ref[k]
    o_ref[...] = x1_ref[...] + mod_ref[5:6, :] * (_rms(mo) * gpost_ref[...])


def _combine(dest_tiles, gates_t, x1, ada3, b_off, seq, g_post, ybuf, td):
    n = x1.shape[0]
    tpb = seq // td
    return pl.pallas_call(
        functools.partial(_combine_kernel, td=td),
        grid=(n // td,),
        in_specs=[pl.BlockSpec((None, 1, TOP_K * td), lambda i: (i, 0, 0), memory_space=pltpu.SMEM),
                  pl.BlockSpec((td, TOP_K), lambda i: (i, 0)),
                  pl.BlockSpec((td, D_MODEL), lambda i: (i, 0)),
                  pl.BlockSpec((None, 6, D_MODEL), lambda i: (i // tpb + b_off, 0, 0)),
                  pl.BlockSpec((1, D_MODEL), lambda i: (0, 0)),
                  pl.BlockSpec(memory_space=pl.ANY)],
        out_specs=pl.BlockSpec((td, D_MODEL), lambda i: (i, 0)),
        out_shape=jax.ShapeDtypeStruct((n, D_MODEL), F32),
        scratch_shapes=[pltpu.VMEM((TOP_K, td, D_MODEL), F32), pltpu.SemaphoreType.DMA],
        compiler_params=_cparams(1),
        name="combine",
    )(dest_tiles, gates_t, x1, ada3, g_post.reshape(1, D_MODEL), ybuf)


def _tile_sizes(nb, seq):
    n = nb * seq
    big = n >= 8192
    return dict(
        tm=min(seq, 256),
        tmm=256 if seq >= 256 else n,
        tq=min(seq, 256),
        tc=CHUNK,
        td=min(seq, 128),
        blk=512 if big else 128,
    )


def _rope_tables(pos):
    half = HEAD_DIM // 2
    inv_freq = ROPE_THETA ** (-jnp.arange(half, dtype=F32) / half)
    ang = pos.astype(F32)[:, None] * inv_freq[None, :]
    cos, sin = jnp.cos(ang), jnp.sin(ang)
    reps = LANES // HEAD_DIM
    return (jnp.tile(jnp.concatenate([cos, cos], axis=1), (1, reps)),
            jnp.tile(jnp.concatenate([-sin, sin], axis=1), (1, reps)))


def _route_plan(idx, rank, counts, n_assign, blk):
    n_blocks = -(-(n_assign + N_EXPERTS * (blk - 1)) // blk)
    padded = (counts + blk - 1) // blk * blk
    pend = jnp.cumsum(padded)
    pstart = pend - padded
    eio = jnp.arange(N_EXPERTS, dtype=I32)[:, None, None]
    dest = jnp.sum(jnp.where(idx[None] == eio, pstart[:, None, None], 0), axis=0) + rank
    blk_expert = jnp.minimum(
        jnp.searchsorted(pend, jnp.arange(n_blocks, dtype=I32) * blk, side='right'), N_EXPERTS - 1).astype(I32)
    n_rows = n_blocks * blk
    pad_start = jnp.concatenate([pstart + counts, pend[-1:]])
    pad_count = jnp.concatenate([padded - counts, n_rows - pend[-1:]])
    return dest.astype(I32), blk_expert, jnp.stack([pad_start, pad_count]).astype(I32), n_rows


def _layer(x, ada3, b_off, pos, k_past, v_past, s0, wts):
    nb, seq, _ = x.shape
    n = nb * seq
    ts = _tile_sizes(nb, seq)
    x2 = x.reshape(n, D_MODEL)
    cos_t, sin_t = _rope_tables(pos)
    q, k, v, hg, z = _proj(x2, ada3, b_off, seq, wts['g_pre_mix'], wts['w_in'], cos_t, sin_t, ts['tm'])
    k3 = k.reshape(nb, seq, KV_WIDTH)
    v3 = v.reshape(nb, seq, KV_WIDTH)
    if k_past is None:
        koff = 0
    else:
        rows = k_past.shape[1]
        koff = rows
        k3 = jnp.concatenate([k_past.reshape(nb, rows, KV_WIDTH), k3], axis=1)
        v3 = jnp.concatenate([v_past.reshape(nb, rows, KV_WIDTH), v3], axis=1)
    y_att = _attn(q.reshape(nb, seq, ATT_WIDTH), k3, v3, wts['sinks'], ts['tq'], koff)
    y_hg, s_new = _hgrn(hg.reshape(nb, seq, 4 * HG_WIDTH), wts['hgrn_lb'], wts['g_hgrn'], s0, ts['tc'])
    x1, h2, idx, gates, rank, cnt = _merge(
        y_att.reshape(n, ATT_WIDTH), y_hg.reshape(n, HG_WIDTH), z, x2, ada3, b_off, seq,
        wts['w_br_attn'], wts['w_br_hgrn'], wts['w_out'], wts['g_post_mix'], wts['g_pre_ffn'],
        wts['w_router_t'], wts['b_router'], ts['tmm'])
    td, blk = ts['td'], ts['blk']
    counts = cnt[:, 0].astype(I32)
    dest, blk_expert, pad_info, n_rows = _route_plan(idx, rank, counts, n * TOP_K, blk)
    dest_tiles = dest.reshape(TOP_K, n // td, td).transpose(1, 0, 2).reshape(n // td, 1, TOP_K * td)
    buf = _dispatch(h2, dest_tiles, pad_info, n_rows, td)
    ybuf = _experts(buf, blk_expert, wts['w_gate_up'], wts['b_gate_up'], wts['w_down'], wts['b_down'], blk)
    out = _combine(dest_tiles, gates.T, x1, ada3, b_off, seq, wts['g_post_ffn'], ybuf, td)
    win = min(WINDOW, k3.shape[1])
    k_new = k3[:, k3.shape[1] - win:].reshape(nb, win, N_KV_HEADS, HEAD_DIM)
    v_new = v3[:, v3.shape[1] - win:].reshape(nb, win, N_KV_HEADS, HEAD_DIM)
    return out.reshape(nb, seq, D_MODEL), k_new, v_new, s_new


def kernel(x_prompt, x_sample, cache_k, cache_v, state_hgrn, c_prompt, c_sample, w_ada, b_ada, g_pre_mix, g_post_mix, g_pre_ffn, g_post_ffn, w_in, attn_sinks, hgrn_lb, g_hgrn, w_br_attn, w_br_hgrn, w_out, w_router, b_router, w_gate_up, b_gate_up, w_down, b_down):
    n_bp = x_prompt.shape[0]
    wts = dict(
        g_pre_mix=g_pre_mix[0], g_post_mix=g_post_mix[0], g_pre_ffn=g_pre_ffn[0], g_post_ffn=g_post_ffn[0],
        w_in=w_in[0].astype(BF16), sinks=attn_sinks[0], hgrn_lb=hgrn_lb, g_hgrn=g_hgrn[0],
        w_br_attn=w_br_attn[0].astype(BF16), w_br_hgrn=w_br_hgrn[0].astype(BF16), w_out=w_out[0].astype(BF16),
        w_router_t=w_router[0].T, b_router=b_router[0],
        w_gate_up=w_gate_up[0].astype(BF16), b_gate_up=b_gate_up[0],
        w_down=w_down[0].astype(BF16), b_down=b_down[0])
    ada = _ada(jnp.concatenate([c_prompt, c_sample], axis=0), w_ada[0], b_ada[0])
    ada3 = ada.reshape(ada.shape[0], 6, D_MODEL)
    pos_p = jnp.arange(x_prompt.shape[1])
    pos_s = PAST_LEN + jnp.arange(x_sample.shape[1])
    ys, ks, vs, ss = _layer(x_sample, ada3, n_bp, pos_s, cache_k[0], cache_v[0], state_hgrn[0], wts)
    yp, kp, vp, sp = _layer(x_prompt, ada3, 0, pos_p, None, None, None, wts)
    return (yp, ys, kp[None], vp[None], sp[None], ks[None], vs[None], ss[None])
```

```python
import functools

import jax
import jax.numpy as jnp
from jax import lax
from jax.experimental import pallas as pl
from jax.experimental.pallas import tpu as pltpu

F32 = jnp.float32
BF16 = jnp.bfloat16
I32 = jnp.int32

D_MODEL = 1024
PAST_LEN = 1024
CHUNK = 64
HEAD_DIM = 64
N_Q_HEADS = 8
N_KV_HEADS = 2
Q_PER_KV = N_Q_HEADS // N_KV_HEADS
ATT_WIDTH = N_Q_HEADS * HEAD_DIM
KV_WIDTH = N_KV_HEADS * HEAD_DIM
WINDOW = 128
ROPE_THETA = 10000.0
HG_HEADS = 4
HG_DK = 128
HG_DV = 128
HG_WIDTH = HG_HEADS * HG_DK
HG_BLOCK = 16
N_EXPERTS = 32
TOP_K = 4
D_FF = 1024
SWIGLU_LIMIT = 7.0
SWIGLU_ALPHA = 1.702
RMS_EPS = 1e-6
N_IN = ATT_WIDTH + 2 * KV_WIDTH + 4 * HG_WIDTH + 2 * D_MODEL
HG_OFF = ATT_WIDTH + 2 * KV_WIDTH
Z_OFF = HG_OFF + 4 * HG_WIDTH

LANES = 128
VMEM_LIMIT = 56 * 1024 * 1024

_NT = (((1,), (1,)), ((), ()))
_TN = (((0,), (0,)), ((), ()))


def _sigmoid(x):
    return 1.0 / (1.0 + jnp.exp(-x))


def _rms(x):
    return x * lax.rsqrt(jnp.mean(x * x, axis=-1, keepdims=True) + RMS_EPS)


def _cparams(n_axes):
    return pltpu.CompilerParams(dimension_semantics=("arbitrary",) * n_axes,
                                vmem_limit_bytes=VMEM_LIMIT)


def _ada_kernel(c_ref, w_ref, b_ref, o_ref):
    c = c_ref[...]
    s = c * _sigmoid(c)
    o_ref[...] = jnp.dot(s, w_ref[...], precision=lax.Precision.HIGHEST,
                         preferred_element_type=F32) + b_ref[...]


def _ada(c, w_ada, b_ada):
    nb = c.shape[0]
    n_out = w_ada.shape[1]
    tn = D_MODEL
    return pl.pallas_call(
        _ada_kernel,
        grid=(n_out // tn,),
        in_specs=[pl.BlockSpec((nb, D_MODEL), lambda j: (0, 0)),
                  pl.BlockSpec((D_MODEL, tn), lambda j: (0, j)),
                  pl.BlockSpec((1, tn), lambda j: (0, j))],
        out_specs=pl.BlockSpec((nb, tn), lambda j: (0, j)),
        out_shape=jax.ShapeDtypeStruct((nb, n_out), F32),
        compiler_params=_cparams(1),
        name="ada",
    )(c, w_ada, b_ada.reshape(1, n_out))


def _proj_kernel(x_ref, mod_ref, g_ref, w_ref, cos_ref, sin_ref,
                 q_ref, k_ref, v_ref, hg_ref, z_ref):
    x = x_ref[...]
    h = _rms(x) * g_ref[...]
    h = h * (1.0 + mod_ref[1:2, :]) + mod_ref[0:1, :]
    hb = h.astype(BF16)
    cos = cos_ref[...]
    sin = sin_ref[...]
    lane = lax.broadcasted_iota(I32, cos.shape, 1)
    first_half = (lane % HEAD_DIM) < (HEAD_DIM // 2)

    def cols(start, width):
        return jnp.dot(hb, w_ref[:, start:start + width], preferred_element_type=F32)

    def rope(p):
        partner = jnp.where(first_half, pltpu.roll(p, LANES - HEAD_DIM // 2, 1),
                            pltpu.roll(p, HEAD_DIM // 2, 1))
        return p * cos + partner * sin

    for j in range(ATT_WIDTH // LANES):
        q = rope(cols(j * LANES, LANES)) * (HEAD_DIM ** -0.5)
        q_ref[:, j * LANES:(j + 1) * LANES] = q.astype(BF16)
    k_ref[...] = rope(cols(ATT_WIDTH, KV_WIDTH))
    v_ref[...] = cols(ATT_WIDTH + KV_WIDTH, KV_WIDTH)
    for j in range(4):
        hg_ref[:, j * HG_WIDTH:(j + 1) * HG_WIDTH] = cols(HG_OFF + j * HG_WIDTH, HG_WIDTH)
    for j in range(4):
        z_ref[:, j * 512:(j + 1) * 512] = cols(Z_OFF + j * 512, 512)


def _proj(x2, ada3, b_off, seq, g_pre, w_in_bf, cos_t, sin_t, tm):
    n = x2.shape[0]
    tpb = seq // tm
    row = lambda i: (i, 0)
    return pl.pallas_call(
        _proj_kernel,
        grid=(n // tm,),
        in_specs=[pl.BlockSpec((tm, D_MODEL), row),
                  pl.BlockSpec((None, 6, D_MODEL), lambda i: (i // tpb + b_off, 0, 0)),
                  pl.BlockSpec((1, D_MODEL), lambda i: (0, 0)),
                  pl.BlockSpec((D_MODEL, N_IN), lambda i: (0, 0)),
                  pl.BlockSpec((tm, LANES), lambda i: (i % tpb, 0)),
                  pl.BlockSpec((tm, LANES), lambda i: (i % tpb, 0))],
        out_specs=[pl.BlockSpec((tm, ATT_WIDTH), row),
                   pl.BlockSpec((tm, KV_WIDTH), row),
                   pl.BlockSpec((tm, KV_WIDTH), row),
                   pl.BlockSpec((tm, 4 * HG_WIDTH), row),
                   pl.BlockSpec((tm, 2 * D_MODEL), row)],
        out_shape=[jax.ShapeDtypeStruct((n, ATT_WIDTH), BF16),
                   jax.ShapeDtypeStruct((n, KV_WIDTH), F32),
                   jax.ShapeDtypeStruct((n, KV_WIDTH), F32),
                   jax.ShapeDtypeStruct((n, 4 * HG_WIDTH), F32),
                   jax.ShapeDtypeStruct((n, 2 * D_MODEL), F32)],
        compiler_params=_cparams(1),
        name="proj",
    )(x2, ada3, g_pre.reshape(1, D_MODEL), w_in_bf, cos_t, sin_t)


def _attn_kernel(sink_ref, q_ref, k_ref, v_ref, o_ref, *, tq, koff):
    t = pl.program_id(1)
    span = WINDOW + CHUNK
    for c in range(tq // CHUNK):
        r = t * tq + c * CHUNK + koff
        ks = pl.multiple_of(jnp.maximum(r - WINDOW, 0), CHUNK)
        kf = k_ref[pl.ds(ks, span), :]
        vf = v_ref[pl.ds(ks, span), :]
        kpos = ks + lax.broadcasted_iota(I32, (1, span), 1)
        valid = kpos < r + CHUNK
        lane = lax.broadcasted_iota(I32, (span, KV_WIDTH), 1)
        top_row = lax.broadcasted_iota(I32, (2 * CHUNK, 1), 0) < CHUNK
        rows = slice(c * CHUNK, (c + 1) * CHUNK)
        for g in range(N_KV_HEADS):
            own = (lane // HEAD_DIM) == g
            k_own = jnp.where(own, kf, 0.0)
            v_own = jnp.where(own, vf, 0.0)
            k_oth = pltpu.roll(k_own, HEAD_DIM, 1)
            v_oth = pltpu.roll(v_own, HEAD_DIM, 1)
            k_par = (k_own, k_oth) if g == 0 else (k_oth, k_own)
            v_par = (v_own, v_oth) if g == 0 else (v_oth, v_own)
            qq = jnp.concatenate([q_ref[rows, (2 * g + i) * LANES:(2 * g + i + 1) * LANES] for i in range(2)], axis=0)
            acc = None
            for par in range(2):
                s = lax.dot_general(qq, k_par[par].astype(BF16), _NT, preferred_element_type=F32)
                s = jnp.where(valid, s, -jnp.inf)
                h_top = Q_PER_KV * g + par
                sink = jnp.where(top_row, sink_ref[h_top], sink_ref[h_top + 2])
                m = jnp.maximum(jnp.max(s, axis=-1, keepdims=True), sink)
                p = jnp.exp(s - m)
                p = p / (jnp.sum(p, axis=-1, keepdims=True) + jnp.exp(sink - m))
                o = jnp.dot(p.astype(BF16), v_par[par].astype(BF16), preferred_element_type=F32)
                acc = o if acc is None else acc + o
            for i in range(2):
                o_ref[rows, (2 * g + i) * LANES:(2 * g + i + 1) * LANES] = (
                    acc[i * CHUNK:(i + 1) * CHUNK].astype(BF16))


def _attn(q3, k3, v3, sinks, tq, koff):
    nb, tl, _ = q3.shape
    tk = k3.shape[1]
    return pl.pallas_call(
        functools.partial(_attn_kernel, tq=tq, koff=koff),
        grid=(nb, tl // tq),
        in_specs=[pl.BlockSpec(memory_space=pltpu.SMEM),
                  pl.BlockSpec((None, tq, ATT_WIDTH), lambda b, t: (b, t, 0)),
                  pl.BlockSpec((None, tk, KV_WIDTH), lambda b, t: (b, 0, 0)),
                  pl.BlockSpec((None, tk, KV_WIDTH), lambda b, t: (b, 0, 0))],
        out_specs=pl.BlockSpec((None, tq, ATT_WIDTH), lambda b, t: (b, t, 0)),
        out_shape=jax.ShapeDtypeStruct((nb, tl, ATT_WIDTH), BF16),
        compiler_params=_cparams(2),
        name="attn",
    )(sinks, q3, k3, v3)


def _hgrn_kernel(*refs, tc, has_s0):
    if has_s0:
        hg_ref, lbp_ref, gh_ref, s0_ref, y_ref, s_out_ref, st_ref = refs
    else:
        hg_ref, lbp_ref, gh_ref, y_ref, s_out_ref, st_ref = refs
        s0_ref = None
    t = pl.program_id(1)

    @pl.when(t == 0)
    def _():
        for h in range(HG_HEADS):
            st_ref[h] = s0_ref[h].T if has_s0 else jnp.zeros((HG_DV, HG_DK), F32)

    l0 = lbp_ref[0:1, :]
    l1 = lbp_ref[1:2, :]
    lm = jnp.maximum(l0, l1)
    e0 = jnp.exp(l0 - lm)
    lb = e0 / (e0 + jnp.exp(l1 - lm))

    hq = hg_ref[:, 0:HG_WIDTH]
    hf = hg_ref[:, HG_WIDTH:2 * HG_WIDTH]
    hi = hg_ref[:, 2 * HG_WIDTH:3 * HG_WIDTH]
    hz = hg_ref[:, 3 * HG_WIDTH:4 * HG_WIDTH]
    f = lb + (1.0 - lb) * _sigmoid(hf)
    q = hq * _sigmoid(hq)
    kk = 1.0 - f
    g = jnp.log(f)

    r16 = lax.broadcasted_iota(I32, g.shape, 0) % HG_BLOCK
    b = g
    suf = g
    for s in (1, 2, 4, 8):
        b = b + jnp.where(r16 >= s, pltpu.roll(b, s, 0), 0.0)
        suf = suf + jnp.where(r16 < HG_BLOCK - s, pltpu.roll(suf, tc - s, 0), 0.0)
    qt = (q * jnp.exp(b)).astype(BF16)
    kt = (kk * jnp.exp(-b)).astype(BF16)
    kd = (kk * jnp.exp(suf - g)).astype(BF16)
    vb = hi.astype(BF16)
    causal = (lax.broadcasted_iota(I32, (HG_BLOCK, HG_BLOCK), 0)
              >= lax.broadcasted_iota(I32, (HG_BLOCK, HG_BLOCK), 1))

    o_heads = []
    for h in range(HG_HEADS):
        cs = slice(h * HG_DK, (h + 1) * HG_DK)
        st = st_ref[h]
        o_blocks = []
        for j in range(tc // HG_BLOCK):
            rs = slice(j * HG_BLOCK, (j + 1) * HG_BLOCK)
            qt_b, kt_b, kd_b, v_b = qt[rs, cs], kt[rs, cs], kd[rs, cs], vb[rs, cs]
            a = lax.dot_general(qt_b, kt_b, _NT, preferred_element_type=F32)
            a = jnp.where(causal, a, 0.0)
            o = (jnp.dot(a.astype(BF16), v_b, preferred_element_type=F32)
                 + lax.dot_general(qt_b, st.astype(BF16), _NT, preferred_element_type=F32))
            o_blocks.append(o)
            dec = jnp.exp(b[j * HG_BLOCK + HG_BLOCK - 1:(j + 1) * HG_BLOCK, cs])
            st = st * dec + lax.dot_general(v_b, kd_b, _TN, preferred_element_type=F32)
        st_ref[h] = st
        o_h = jnp.concatenate(o_blocks, axis=0)
        zs = hz[:, cs]
        o_heads.append(_rms(o_h) * gh_ref[:, cs] * (zs * _sigmoid(zs)))
    y_ref[...] = jnp.concatenate(o_heads, axis=1).astype(BF16)

    @pl.when(t == pl.num_programs(1) - 1)
    def _():
        for h in range(HG_HEADS):
            s_out_ref[h] = st_ref[h].T


def _hgrn(hg3, hgrn_lb, g_hgrn, s0, tc):
    nb, tl, _ = hg3.shape
    has_s0 = s0 is not None
    st_spec = pl.BlockSpec((None, HG_HEADS, HG_DK, HG_DV), lambda b, t: (b, 0, 0, 0))
    in_specs = [pl.BlockSpec((None, tc, 4 * HG_WIDTH), lambda b, t: (b, t, 0)),
                pl.BlockSpec((2, HG_WIDTH), lambda b, t: (0, 0)),
                pl.BlockSpec((1, HG_WIDTH), lambda b, t: (0, 0))]
    args = [hg3, hgrn_lb, g_hgrn.reshape(1, HG_WIDTH)]
    if has_s0:
        in_specs.append(st_spec)
        args.append(s0)
    return pl.pallas_call(
        functools.partial(_hgrn_kernel, tc=tc, has_s0=has_s0),
        grid=(nb, tl // tc),
        in_specs=in_specs,
        out_specs=[pl.BlockSpec((None, tc, HG_WIDTH), lambda b, t: (b, t, 0)), st_spec],
        out_shape=[jax.ShapeDtypeStruct((nb, tl, HG_WIDTH), BF16),
                   jax.ShapeDtypeStruct((nb, HG_HEADS, HG_DK, HG_DV), F32)],
        scratch_shapes=[pltpu.VMEM((HG_HEADS, HG_DV, HG_DK), F32)],
        compiler_params=_cparams(2),
        name="hgrn",
    )(*args)


def _merge_kernel(a_ref, b_ref, z_ref, x_ref, mod_ref, wa_ref, wb_ref, wo_ref, gpost_ref, gpre_ref,
                  wrh_ref, wrl_ref, br_ref, x1_ref, h2_ref, idx_ref, gate_ref, rank_ref, cnt_ref, carry_ref,
                  *, tm, nseg, sub):
    i = pl.program_id(0)
    seg = tm // nseg

    def mod_rows(r, part):
        if nseg == 1:
            return mod_ref[0, r:r + 1, :]
        segs = range(part * sub // seg, (part + 1) * sub // seg)
        return jnp.concatenate([jnp.broadcast_to(mod_ref[s, r:r + 1, :], (seg, D_MODEL)) for s in segs], axis=0)

    @pl.when(i == 0)
    def _():
        carry_ref[...] = jnp.zeros_like(carry_ref)

    earlier = (lax.broadcasted_iota(I32, (sub, sub), 0) < lax.broadcasted_iota(I32, (sub, sub), 1)).astype(BF16)
    eio = lax.broadcasted_iota(I32, (N_EXPERTS, sub), 0).astype(F32)
    carry = carry_ref[:, 0:1]
    for part in range(tm // sub):
        rs = slice(part * sub, (part + 1) * sub)
        za = z_ref[rs, 0:D_MODEL]
        zh = z_ref[rs, D_MODEL:2 * D_MODEL]
        m = (_sigmoid(za) * jnp.dot(a_ref[rs, :], wa_ref[...], preferred_element_type=F32)
             + _sigmoid(zh) * jnp.dot(b_ref[rs, :], wb_ref[...], preferred_element_type=F32))
        y = jnp.dot(m.astype(BF16), wo_ref[...], preferred_element_type=F32)
        x1 = x_ref[rs, :] + mod_rows(2, part) * (_rms(y) * gpost_ref[...])
        x1_ref[rs, :] = x1
        h2 = (_rms(x1) * gpre_ref[...]) * (1.0 + mod_rows(4, part)) + mod_rows(3, part)
        h2_ref[rs, :] = h2

        h_hi = h2.astype(BF16)
        h_lo = (h2 - h_hi.astype(F32)).astype(BF16)
        lg = (jnp.dot(h_hi, wrh_ref[...], preferred_element_type=F32)
              + jnp.dot(h_hi, wrl_ref[...], preferred_element_type=F32)
              + jnp.dot(h_lo, wrh_ref[...], preferred_element_type=F32))
        logits = lg.T[0:N_EXPERTS, :] + br_ref[...]
        vals, sels = [], []
        cur = logits
        for k in range(TOP_K):
            mx = jnp.max(cur, axis=0, keepdims=True)
            ik = jnp.min(jnp.where(cur == mx, eio, float(N_EXPERTS)), axis=0, keepdims=True)
            sel = eio == ik
            idx_ref[k:k + 1, rs] = ik.astype(I32)
            vals.append(mx)
            sels.append(sel)
            cur = jnp.where(sel, -jnp.inf, cur)
        es = [jnp.exp(v - vals[0]) for v in vals]
        den = es[0] + es[1] + es[2] + es[3]
        for k in range(TOP_K):
            gate_ref[k:k + 1, rs] = es[k] / den

        onehot = jnp.zeros(logits.shape, F32)
        for k in range(TOP_K):
            onehot = onehot + sels[k].astype(F32)
        base = jnp.dot(onehot.astype(BF16), earlier, preferred_element_type=F32) + carry
        for k in range(TOP_K):
            rank_ref[k:k + 1, rs] = jnp.sum(jnp.where(sels[k], base, 0.0), axis=0, keepdims=True).astype(I32)
        carry = carry + jnp.sum(onehot, axis=1, keepdims=True)
    carry_ref[...] = jnp.broadcast_to(carry, carry_ref.shape)
    cnt_ref[...] = jnp.broadcast_to(carry, cnt_ref.shape)


def _merge(a2, b2, z2, x2, ada3, b_off, seq, wa, wb, wo, g_post, g_pre, wr_hi, wr_lo, br, tm):
    n = x2.shape[0]
    sub = min(tm, 256)
    if tm >= seq:
        nseg = tm // seq
        assert b_off % nseg == 0
        mod_idx = lambda i: (i + b_off // nseg, 0, 0)
    else:
        nseg = 1
        tpb = seq // tm
        mod_idx = lambda i: (i // tpb + b_off, 0, 0)
    row = lambda i: (i, 0)
    const = lambda i: (0, 0)
    col = lambda i: (0, i)
    return pl.pallas_call(
        functools.partial(_merge_kernel, tm=tm, nseg=nseg, sub=sub),
        grid=(n // tm,),
        in_specs=[pl.BlockSpec((tm, ATT_WIDTH), row),
                  pl.BlockSpec((tm, HG_WIDTH), row),
                  pl.BlockSpec((tm, 2 * D_MODEL), row),
                  pl.BlockSpec((tm, D_MODEL), row),
                  pl.BlockSpec((nseg, 6, D_MODEL), mod_idx),
                  pl.BlockSpec((ATT_WIDTH, D_MODEL), const),
                  pl.BlockSpec((HG_WIDTH, D_MODEL), const),
                  pl.BlockSpec((D_MODEL, D_MODEL), const),
                  pl.BlockSpec((1, D_MODEL), const),
                  pl.BlockSpec((1, D_MODEL), const),
                  pl.BlockSpec((D_MODEL, LANES), const),
                  pl.BlockSpec((D_MODEL, LANES), const),
                  pl.BlockSpec((N_EXPERTS, 1), const)],
        out_specs=[pl.BlockSpec((tm, D_MODEL), row),
                   pl.BlockSpec((tm, D_MODEL), row),
                   pl.BlockSpec((TOP_K, tm), col),
                   pl.BlockSpec((TOP_K, tm), col),
                   pl.BlockSpec((TOP_K, tm), col),
                   pl.BlockSpec((N_EXPERTS, LANES), const)],
        out_shape=[jax.ShapeDtypeStruct((n, D_MODEL), F32),
                   jax.ShapeDtypeStruct((n, D_MODEL), F32),
                   jax.ShapeDtypeStruct((TOP_K, n), I32),
                   jax.ShapeDtypeStruct((TOP_K, n), F32),
                   jax.ShapeDtypeStruct((TOP_K, n), I32),
                   jax.ShapeDtypeStruct((N_EXPERTS, LANES), F32)],
        scratch_shapes=[pltpu.VMEM((N_EXPERTS, LANES), F32)],
        compiler_params=_cparams(1),
        name="merge",
    )(a2, b2, z2, x2, ada3, wa, wb, wo, g_post.reshape(1, D_MODEL), g_pre.reshape(1, D_MODEL),
      wr_hi, wr_lo, br.reshape(N_EXPERTS, 1))


def _dispatch_kernel(dest_ref, pad_ref, h_ref, buf_ref, zero_ref, sem, zsem, *, td):
    i = pl.program_id(0)

    def row_copy(t, d):
        return pltpu.make_async_copy(h_ref.at[pl.ds(t, 1)], buf_ref.at[pl.ds(d, 1)], sem)

    def zero_copy(d):
        return pltpu.make_async_copy(zero_ref, buf_ref.at[pl.ds(d, 1)], zsem)

    @pl.when(i == 0)
    def _():
        zero_ref[...] = jnp.zeros_like(zero_ref)
        for e in range(N_EXPERTS + 1):
            start = pad_ref[0, e]
            count = pad_ref[1, e]

            def zstart(r, _):
                zero_copy(start + r).start()
                return 0

            def zwait(r, _):
                zero_copy(start + r).wait()
                return 0

            lax.fori_loop(0, count, zstart, 0)
            lax.fori_loop(0, count, zwait, 0)

    def issue(t, _):
        for k in range(TOP_K):
            row_copy(t, dest_ref[0, k * td + t]).start(priority=k % 2)
        return 0

    lax.fori_loop(0, td, issue, 0, unroll=8)
    for k in range(TOP_K):
        pltpu.make_async_copy(h_ref, buf_ref.at[pl.ds(0, td)], sem).wait()


def _dispatch(h2, dest_tiles, pad_info, n_rows, td):
    n = h2.shape[0]
    return pl.pallas_call(
        functools.partial(_dispatch_kernel, td=td),
        grid=(n // td,),
        in_specs=[pl.BlockSpec((None, 1, TOP_K * td), lambda i: (i, 0, 0), memory_space=pltpu.SMEM),
                  pl.BlockSpec(memory_space=pltpu.SMEM),
                  pl.BlockSpec((td, D_MODEL), lambda i: (i, 0))],
        out_specs=pl.BlockSpec(memory_space=pl.ANY),
        out_shape=jax.ShapeDtypeStruct((n_rows, D_MODEL), F32),
        scratch_shapes=[pltpu.VMEM((1, D_MODEL), F32), pltpu.SemaphoreType.DMA, pltpu.SemaphoreType.DMA],
        compiler_params=_cparams(1),
        name="dispatch",
    )(dest_tiles, pad_info, h2)


def _expert_kernel(be_ref, x_ref, wgu_ref, bgu_ref, wd_ref, bd_ref, y_ref):
    del be_ref
    gu = jnp.dot(x_ref[...].astype(BF16), wgu_ref[...], preferred_element_type=F32) + bgu_ref[...]
    gate = jnp.minimum(gu[:, :D_FF], SWIGLU_LIMIT)
    up = jnp.clip(gu[:, D_FF:], -SWIGLU_LIMIT, SWIGLU_LIMIT)
    act = gate * _sigmoid(SWIGLU_ALPHA * gate) * (up + 1.0)
    y_ref[...] = jnp.dot(act.astype(BF16), wd_ref[...], preferred_element_type=F32) + bd_ref[...]


def _experts(buf, blk_expert, wgu_bf, bgu, wd_bf, bd, blk):
    n_rows = buf.shape[0]
    grid_spec = pltpu.PrefetchScalarGridSpec(
        num_scalar_prefetch=1,
        grid=(n_rows // blk,),
        in_specs=[pl.BlockSpec((blk, D_MODEL), lambda i, be: (i, 0)),
                  pl.BlockSpec((None, D_MODEL, 2 * D_FF), lambda i, be: (be[i], 0, 0)),
                  pl.BlockSpec((None, 1, 2 * D_FF), lambda i, be: (be[i], 0, 0)),
                  pl.BlockSpec((None, D_FF, D_MODEL), lambda i, be: (be[i], 0, 0)),
                  pl.BlockSpec((None, 1, D_MODEL), lambda i, be: (be[i], 0, 0))],
        out_specs=pl.BlockSpec((blk, D_MODEL), lambda i, be: (i, 0)),
    )
    return pl.pallas_call(
        _expert_kernel,
        grid_spec=grid_spec,
        out_shape=jax.ShapeDtypeStruct((n_rows, D_MODEL), F32),
        compiler_params=_cparams(1),
        name="experts",
    )(blk_expert, buf, wgu_bf, bgu.reshape(N_EXPERTS, 1, 2 * D_FF), wd_bf, bd.reshape(N_EXPERTS, 1, D_MODEL))


def _combine_kernel(dest_ref, dnext_ref, gate_ref, x1_ref, mod_ref, gpost_ref, ybuf_ref, o_ref, rows_ref, sem, *, td):
    i = pl.program_id(0)
    slot = i % 2

    def gather_tile(d_ref, s):
        def issue(t, _):
            for k in range(TOP_K):
                pltpu.make_async_copy(ybuf_ref.at[pl.ds(d_ref[0, k * td + t], 1)],
                                      rows_ref.at[s, k, pl.ds(t, 1)], sem.at[s]).start(priority=k % 2)
            return 0

        lax.fori_loop(0, td, issue, 0, unroll=8)

    @pl.when(i == 0)
    def _():
        gather_tile(dest_ref, 0)

    @pl.when(i + 1 < pl.num_programs(0))
    def _():
        gather_tile(dnext_ref, 1 - slot)

    for k in range(TOP_K):
        pltpu.make_async_copy(ybuf_ref.at[pl.ds(0, td)], rows_ref.at[slot, k], sem.at[slot]).wait()
    mo = gate_ref[:, 0:1] * rows_ref[slot, 0]
    for k in range(1, TOP_K):
        mo = mo + gate_ref[:, k:k + 1] * rows_ref[slot, k]
    o_ref[...] = x1_ref[...] + mod_ref[5:6, :] * (_rms(mo) * gpost_ref[...])


def _combine(dest_tiles, gates_t, x1, ada3, b_off, seq, g_post, ybuf, td):
    n = x1.shape[0]
    tpb = seq // td
    n_tiles = n // td
    return pl.pallas_call(
        functools.partial(_combine_kernel, td=td),
        grid=(n_tiles,),
        in_specs=[pl.BlockSpec((None, 1, TOP_K * td), lambda i: (i, 0, 0), memory_space=pltpu.SMEM),
                  pl.BlockSpec((None, 1, TOP_K * td), lambda i: (jnp.minimum(i + 1, n_tiles - 1), 0, 0),
                               memory_space=pltpu.SMEM),
                  pl.BlockSpec((td, TOP_K), lambda i: (i, 0)),
                  pl.BlockSpec((td, D_MODEL), lambda i: (i, 0)),
                  pl.BlockSpec((None, 6, D_MODEL), lambda i: (i // tpb + b_off, 0, 0)),
                  pl.BlockSpec((1, D_MODEL), lambda i: (0, 0)),
                  pl.BlockSpec(memory_space=pl.ANY)],
        out_specs=pl.BlockSpec((td, D_MODEL), lambda i: (i, 0)),
        out_shape=jax.ShapeDtypeStruct((n, D_MODEL), F32),
        scratch_shapes=[pltpu.VMEM((2, TOP_K, td, D_MODEL), F32), pltpu.SemaphoreType.DMA((2,))],
        compiler_params=_cparams(1),
        name="combine",
    )(dest_tiles, dest_tiles, gates_t, x1, ada3, g_post.reshape(1, D_MODEL), ybuf)


def _tile_sizes(nb, seq):
    n = nb * seq
    big = n >= 8192
    return dict(
        tm=min(seq, 256),
        tmm=512 if seq >= 512 else n,
        tq=min(seq, 256),
        tc=min(seq, 256),
        td=min(seq, 256),
        blk=512 if big else 128,
    )


def _rope_tables(pos):
    half = HEAD_DIM // 2
    inv_freq = ROPE_THETA ** (-jnp.arange(half, dtype=F32) / half)
    ang = pos.astype(F32)[:, None] * inv_freq[None, :]
    cos, sin = jnp.cos(ang), jnp.sin(ang)
    reps = LANES // HEAD_DIM
    return (jnp.tile(jnp.concatenate([cos, cos], axis=1), (1, reps)),
            jnp.tile(jnp.concatenate([-sin, sin], axis=1), (1, reps)))


def _route_plan(idx, rank, counts, n_assign, blk):
    n_blocks = -(-(n_assign + N_EXPERTS * (blk - 1)) // blk)
    padded = (counts + blk - 1) // blk * blk
    pend = jnp.cumsum(padded)
    pstart = pend - padded
    eio = jnp.arange(N_EXPERTS, dtype=I32)[:, None, None]
    dest = jnp.sum(jnp.where(idx[None] == eio, pstart[:, None, None], 0), axis=0) + rank
    blk_row = jnp.arange(n_blocks, dtype=I32) * blk
    blk_expert = jnp.minimum(jnp.sum((pend[None, :] <= blk_row[:, None]).astype(I32), axis=1), N_EXPERTS - 1)
    n_rows = n_blocks * blk
    pad_start = jnp.concatenate([pstart + counts, pend[-1:]])
    pad_count = jnp.concatenate([padded - counts, n_rows - pend[-1:]])
    return dest.astype(I32), blk_expert, jnp.stack([pad_start, pad_count]).astype(I32), n_rows


def _layer(x, ada3, b_off, pos, k_past, v_past, s0, wts):
    nb, seq, _ = x.shape
    n = nb * seq
    ts = _tile_sizes(nb, seq)
    x2 = x.reshape(n, D_MODEL)
    cos_t, sin_t = _rope_tables(pos)
    q, k, v, hg, z = _proj(x2, ada3, b_off, seq, wts['g_pre_mix'], wts['w_in'], cos_t, sin_t, ts['tm'])
    k3 = k.reshape(nb, seq, KV_WIDTH)
    v3 = v.reshape(nb, seq, KV_WIDTH)
    if k_past is None:
        koff = 0
    else:
        rows = k_past.shape[1]
        koff = rows
        k3 = jnp.concatenate([k_past.reshape(nb, rows, KV_WIDTH), k3], axis=1)
        v3 = jnp.concatenate([v_past.reshape(nb, rows, KV_WIDTH), v3], axis=1)
    y_att = _attn(q.reshape(nb, seq, ATT_WIDTH), k3, v3, wts['sinks'], ts['tq'], koff)
    y_hg, s_new = _hgrn(hg.reshape(nb, seq, 4 * HG_WIDTH), wts['hgrn_lb'], wts['g_hgrn'], s0, ts['tc'])
    x1, h2, idx, gates, rank, cnt = _merge(
        y_att.reshape(n, ATT_WIDTH), y_hg.reshape(n, HG_WIDTH), z, x2, ada3, b_off, seq,
        wts['w_br_attn'], wts['w_br_hgrn'], wts['w_out'], wts['g_post_mix'], wts['g_pre_ffn'],
        wts['w_router_hi'], wts['w_router_lo'], wts['b_router'], ts['tmm'])
    td, blk = ts['td'], ts['blk']
    counts = cnt[:, 0].astype(I32)
    dest, blk_expert, pad_info, n_rows = _route_plan(idx, rank, counts, n * TOP_K, blk)
    dest_tiles = dest.reshape(TOP_K, n // td, td).transpose(1, 0, 2).reshape(n // td, 1, TOP_K * td)
    buf = _dispatch(h2, dest_tiles, pad_info, n_rows, td)
    ybuf = _experts(buf, blk_expert, wts['w_gate_up'], wts['b_gate_up'], wts['w_down'], wts['b_down'], blk)
    out = _combine(dest_tiles, gates.T, x1, ada3, b_off, seq, wts['g_post_ffn'], ybuf, td)
    win = min(WINDOW, k3.shape[1])
    k_new = k3[:, k3.shape[1] - win:].reshape(nb, win, N_KV_HEADS, HEAD_DIM)
    v_new = v3[:, v3.shape[1] - win:].reshape(nb, win, N_KV_HEADS, HEAD_DIM)
    return out.reshape(nb, seq, D_MODEL), k_new, v_new, s_new


def kernel(x_prompt, x_sample, cache_k, cache_v, state_hgrn, c_prompt, c_sample, w_ada, b_ada, g_pre_mix, g_post_mix, g_pre_ffn, g_post_ffn, w_in, attn_sinks, hgrn_lb, g_hgrn, w_br_attn, w_br_hgrn, w_out, w_router, b_router, w_gate_up, b_gate_up, w_down, b_down):
    n_bp = x_prompt.shape[0]
    wr = jnp.pad(w_router[0], ((0, 0), (0, LANES - N_EXPERTS)))
    wr_hi = wr.astype(BF16)
    wr_lo = (wr - wr_hi.astype(F32)).astype(BF16)
    wts = dict(
        g_pre_mix=g_pre_mix[0], g_post_mix=g_post_mix[0], g_pre_ffn=g_pre_ffn[0], g_post_ffn=g_post_ffn[0],
        w_in=w_in[0].astype(BF16), sinks=attn_sinks[0], hgrn_lb=hgrn_lb, g_hgrn=g_hgrn[0],
        w_br_attn=w_br_attn[0].astype(BF16), w_br_hgrn=w_br_hgrn[0].astype(BF16), w_out=w_out[0].astype(BF16),
        w_router_hi=wr_hi, w_router_lo=wr_lo, b_router=b_router[0],
        w_gate_up=w_gate_up[0].astype(BF16), b_gate_up=b_gate_up[0],
        w_down=w_down[0].astype(BF16), b_down=b_down[0])
    ada = _ada(jnp.concatenate([c_prompt, c_sample], axis=0), w_ada[0], b_ada[0])
    ada3 = ada.reshape(ada.shape[0], 6, D_MODEL)
    pos_p = jnp.arange(x_prompt.shape[1])
    pos_s = PAST_LEN + jnp.arange(x_sample.shape[1])
    ys, ks, vs, ss = _layer(x_sample, ada3, n_bp, pos_s, cache_k[0], cache_v[0], state_hgrn[0], wts)
    yp, kp, vp, sp = _layer(x_prompt, ada3, 0, pos_p, None, None, None, wts)
    return (yp, ys, kp[None], vp[None], sp[None], ks[None], vs[None], ss[None])
```

```python
import functools

import jax
import jax.numpy as jnp
from jax import lax
from jax.experimental import pallas as pl
from jax.experimental.pallas import tpu as pltpu

F32 = jnp.float32
BF16 = jnp.bfloat16
I32 = jnp.int32

D_MODEL = 1024
PAST_LEN = 1024
CHUNK = 64
HEAD_DIM = 64
N_Q_HEADS = 8
N_KV_HEADS = 2
Q_PER_KV = N_Q_HEADS // N_KV_HEADS
ATT_WIDTH = N_Q_HEADS * HEAD_DIM
KV_WIDTH = N_KV_HEADS * HEAD_DIM
WINDOW = 128
ROPE_THETA = 10000.0
HG_HEADS = 4
HG_DK = 128
HG_DV = 128
HG_WIDTH = HG_HEADS * HG_DK
HG_BLOCK = 16
N_EXPERTS = 32
TOP_K = 4
D_FF = 1024
SWIGLU_LIMIT = 7.0
SWIGLU_ALPHA = 1.702
RMS_EPS = 1e-6
N_IN = ATT_WIDTH + 2 * KV_WIDTH + 4 * HG_WIDTH + 2 * D_MODEL
HG_OFF = ATT_WIDTH + 2 * KV_WIDTH
Z_OFF = HG_OFF + 4 * HG_WIDTH

LANES = 128
SUBLANES = 8
VMEM_LIMIT = 56 * 1024 * 1024

_NT = (((1,), (1,)), ((), ()))
_TN = (((0,), (0,)), ((), ()))


def _sigmoid(x):
    return 1.0 / (1.0 + jnp.exp(-x))


def _rms(x):
    return x * lax.rsqrt(jnp.mean(x * x, axis=-1, keepdims=True) + RMS_EPS)


def _cparams(n_axes):
    return pltpu.CompilerParams(dimension_semantics=("arbitrary",) * n_axes,
                                vmem_limit_bytes=VMEM_LIMIT)


def _ada_kernel(c_ref, w_ref, b_ref, o_ref):
    c = c_ref[...]
    s = c * _sigmoid(c)
    o_ref[...] = jnp.dot(s, w_ref[...], precision=lax.Precision.HIGHEST,
                         preferred_element_type=F32) + b_ref[...]


def _ada(c, w_ada, b_ada):
    nb = c.shape[0]
    n_out = w_ada.shape[1]
    tn = D_MODEL
    return pl.pallas_call(
        _ada_kernel,
        grid=(n_out // tn,),
        in_specs=[pl.BlockSpec((nb, D_MODEL), lambda j: (0, 0)),
                  pl.BlockSpec((D_MODEL, tn), lambda j: (0, j)),
                  pl.BlockSpec((1, tn), lambda j: (0, j))],
        out_specs=pl.BlockSpec((nb, tn), lambda j: (0, j)),
        out_shape=jax.ShapeDtypeStruct((nb, n_out), F32),
        compiler_params=_cparams(1),
        name="ada",
    )(c, w_ada, b_ada.reshape(1, n_out))


def _proj_kernel(x_ref, mod_ref, g_ref, w_ref, cos_ref, sin_ref,
                 q_ref, k_ref, v_ref, hg_ref, z_ref):
    x = x_ref[...]
    h = _rms(x) * g_ref[...]
    h = h * (1.0 + mod_ref[1:2, :]) + mod_ref[0:1, :]
    hb = h.astype(BF16)
    cos = cos_ref[...]
    sin = sin_ref[...]
    lane = lax.broadcasted_iota(I32, cos.shape, 1)
    first_half = (lane % HEAD_DIM) < (HEAD_DIM // 2)

    def cols(start, width):
        return jnp.dot(hb, w_ref[:, start:start + width], preferred_element_type=F32)

    def rope(p):
        partner = jnp.where(first_half, pltpu.roll(p, LANES - HEAD_DIM // 2, 1),
                            pltpu.roll(p, HEAD_DIM // 2, 1))
        return p * cos + partner * sin

    for j in range(ATT_WIDTH // LANES):
        q = rope(cols(j * LANES, LANES)) * (HEAD_DIM ** -0.5)
        q_ref[:, j * LANES:(j + 1) * LANES] = q.astype(BF16)
    k_ref[...] = rope(cols(ATT_WIDTH, KV_WIDTH))
    v_ref[...] = cols(ATT_WIDTH + KV_WIDTH, KV_WIDTH)
    for j in range(4):
        hg_ref[:, j * HG_WIDTH:(j + 1) * HG_WIDTH] = cols(HG_OFF + j * HG_WIDTH, HG_WIDTH)
    for j in range(4):
        z_ref[:, j * 512:(j + 1) * 512] = cols(Z_OFF + j * 512, 512)


def _proj(x2, ada3, b_off, seq, g_pre, w_in_bf, cos_t, sin_t, tm):
    n = x2.shape[0]
    tpb = seq // tm
    row = lambda i: (i, 0)
    return pl.pallas_call(
        _proj_kernel,
        grid=(n // tm,),
        in_specs=[pl.BlockSpec((tm, D_MODEL), row),
                  pl.BlockSpec((None, 6, D_MODEL), lambda i: (i // tpb + b_off, 0, 0)),
                  pl.BlockSpec((1, D_MODEL), lambda i: (0, 0)),
                  pl.BlockSpec((D_MODEL, N_IN), lambda i: (0, 0)),
                  pl.BlockSpec((tm, LANES), lambda i: (i % tpb, 0)),
                  pl.BlockSpec((tm, LANES), lambda i: (i % tpb, 0))],
        out_specs=[pl.BlockSpec((tm, ATT_WIDTH), row),
                   pl.BlockSpec((tm, KV_WIDTH), row),
                   pl.BlockSpec((tm, KV_WIDTH), row),
                   pl.BlockSpec((tm, 4 * HG_WIDTH), row),
                   pl.BlockSpec((tm, 2 * D_MODEL), row)],
        out_shape=[jax.ShapeDtypeStruct((n, ATT_WIDTH), BF16),
                   jax.ShapeDtypeStruct((n, KV_WIDTH), F32),
                   jax.ShapeDtypeStruct((n, KV_WIDTH), F32),
                   jax.ShapeDtypeStruct((n, 4 * HG_WIDTH), F32),
                   jax.ShapeDtypeStruct((n, 2 * D_MODEL), F32)],
        compiler_params=_cparams(1),
        name="proj",
    )(x2, ada3, g_pre.reshape(1, D_MODEL), w_in_bf, cos_t, sin_t)


def _attn_kernel(sink_ref, q_ref, k_ref, v_ref, o_ref, *, tq, koff):
    t = pl.program_id(1)
    span = WINDOW + CHUNK
    for c in range(tq // CHUNK):
        r = t * tq + c * CHUNK + koff
        ks = pl.multiple_of(jnp.maximum(r - WINDOW, 0), CHUNK)
        kf = k_ref[pl.ds(ks, span), :]
        vf = v_ref[pl.ds(ks, span), :]
        kpos = ks + lax.broadcasted_iota(I32, (1, span), 1)
        valid = kpos < r + CHUNK
        lane = lax.broadcasted_iota(I32, (span, KV_WIDTH), 1)
        top_row = lax.broadcasted_iota(I32, (2 * CHUNK, 1), 0) < CHUNK
        rows = slice(c * CHUNK, (c + 1) * CHUNK)
        for g in range(N_KV_HEADS):
            own = (lane // HEAD_DIM) == g
            k_own = jnp.where(own, kf, 0.0)
            v_own = jnp.where(own, vf, 0.0)
            k_oth = pltpu.roll(k_own, HEAD_DIM, 1)
            v_oth = pltpu.roll(v_own, HEAD_DIM, 1)
            k_par = (k_own, k_oth) if g == 0 else (k_oth, k_own)
            v_par = (v_own, v_oth) if g == 0 else (v_oth, v_own)
            qq = jnp.concatenate([q_ref[rows, (2 * g + i) * LANES:(2 * g + i + 1) * LANES] for i in range(2)], axis=0)
            acc = None
            for par in range(2):
                s = lax.dot_general(qq, k_par[par].astype(BF16), _NT, preferred_element_type=F32)
                s = jnp.where(valid, s, -jnp.inf)
                h_top = Q_PER_KV * g + par
                sink = jnp.where(top_row, sink_ref[h_top], sink_ref[h_top + 2])
                m = jnp.maximum(jnp.max(s, axis=-1, keepdims=True), sink)
                p = jnp.exp(s - m)
                p = p / (jnp.sum(p, axis=-1, keepdims=True) + jnp.exp(sink - m))
                o = jnp.dot(p.astype(BF16), v_par[par].astype(BF16), preferred_element_type=F32)
                acc = o if acc is None else acc + o
            for i in range(2):
                o_ref[rows, (2 * g + i) * LANES:(2 * g + i + 1) * LANES] = (
                    acc[i * CHUNK:(i + 1) * CHUNK].astype(BF16))


def _attn(q3, k3, v3, sinks, tq, koff):
    nb, tl, _ = q3.shape
    tk = k3.shape[1]
    return pl.pallas_call(
        functools.partial(_attn_kernel, tq=tq, koff=koff),
        grid=(nb, tl // tq),
        in_specs=[pl.BlockSpec(memory_space=pltpu.SMEM),
                  pl.BlockSpec((None, tq, ATT_WIDTH), lambda b, t: (b, t, 0)),
                  pl.BlockSpec((None, tk, KV_WIDTH), lambda b, t: (b, 0, 0)),
                  pl.BlockSpec((None, tk, KV_WIDTH), lambda b, t: (b, 0, 0))],
        out_specs=pl.BlockSpec((None, tq, ATT_WIDTH), lambda b, t: (b, t, 0)),
        out_shape=jax.ShapeDtypeStruct((nb, tl, ATT_WIDTH), BF16),
        compiler_params=_cparams(2),
        name="attn",
    )(sinks, q3, k3, v3)


def _hgrn_kernel(*refs, tc, has_s0):
    if has_s0:
        hg_ref, lbp_ref, gh_ref, s0_ref, y_ref, s_out_ref, st_ref = refs
    else:
        hg_ref, lbp_ref, gh_ref, y_ref, s_out_ref, st_ref = refs
        s0_ref = None
    t = pl.program_id(1)

    @pl.when(t == 0)
    def _():
        for h in range(HG_HEADS):
            st_ref[h] = s0_ref[h].T if has_s0 else jnp.zeros((HG_DV, HG_DK), F32)

    l0 = lbp_ref[0:1, :]
    l1 = lbp_ref[1:2, :]
    lm = jnp.maximum(l0, l1)
    e0 = jnp.exp(l0 - lm)
    lb = e0 / (e0 + jnp.exp(l1 - lm))

    hq = hg_ref[:, 0:HG_WIDTH]
    hf = hg_ref[:, HG_WIDTH:2 * HG_WIDTH]
    hi = hg_ref[:, 2 * HG_WIDTH:3 * HG_WIDTH]
    hz = hg_ref[:, 3 * HG_WIDTH:4 * HG_WIDTH]
    f = lb + (1.0 - lb) * _sigmoid(hf)
    q = hq * _sigmoid(hq)
    kk = 1.0 - f
    g = jnp.log(f)

    r16 = lax.broadcasted_iota(I32, g.shape, 0) % HG_BLOCK
    b = g
    suf = g
    for s in (1, 2, 4, 8):
        b = b + jnp.where(r16 >= s, pltpu.roll(b, s, 0), 0.0)
        suf = suf + jnp.where(r16 < HG_BLOCK - s, pltpu.roll(suf, tc - s, 0), 0.0)
    qt = (q * jnp.exp(b)).astype(BF16)
    kt = (kk * jnp.exp(-b)).astype(BF16)
    kd = (kk * jnp.exp(suf - g)).astype(BF16)
    vb = hi.astype(BF16)
    causal = (lax.broadcasted_iota(I32, (HG_BLOCK, HG_BLOCK), 0)
              >= lax.broadcasted_iota(I32, (HG_BLOCK, HG_BLOCK), 1))

    o_heads = []
    for h in range(HG_HEADS):
        cs = slice(h * HG_DK, (h + 1) * HG_DK)
        st = st_ref[h]
        o_blocks = []
        for j in range(tc // HG_BLOCK):
            rs = slice(j * HG_BLOCK, (j + 1) * HG_BLOCK)
            qt_b, kt_b, kd_b, v_b = qt[rs, cs], kt[rs, cs], kd[rs, cs], vb[rs, cs]
            a = lax.dot_general(qt_b, kt_b, _NT, preferred_element_type=F32)
            a = jnp.where(causal, a, 0.0)
            o = (jnp.dot(a.astype(BF16), v_b, preferred_element_type=F32)
                 + lax.dot_general(qt_b, st.astype(BF16), _NT, preferred_element_type=F32))
            o_blocks.append(o)
            dec = jnp.exp(b[j * HG_BLOCK + HG_BLOCK - 1:(j + 1) * HG_BLOCK, cs])
            st = st * dec + lax.dot_general(v_b, kd_b, _TN, preferred_element_type=F32)
        st_ref[h] = st
        o_h = jnp.concatenate(o_blocks, axis=0)
        zs = hz[:, cs]
        o_heads.append(_rms(o_h) * gh_ref[:, cs] * (zs * _sigmoid(zs)))
    y_ref[...] = jnp.concatenate(o_heads, axis=1).astype(BF16)

    @pl.when(t == pl.num_programs(1) - 1)
    def _():
        for h in range(HG_HEADS):
            s_out_ref[h] = st_ref[h].T


def _hgrn(hg3, hgrn_lb, g_hgrn, s0, tc):
    nb, tl, _ = hg3.shape
    has_s0 = s0 is not None
    st_spec = pl.BlockSpec((None, HG_HEADS, HG_DK, HG_DV), lambda b, t: (b, 0, 0, 0))
    in_specs = [pl.BlockSpec((None, tc, 4 * HG_WIDTH), lambda b, t: (b, t, 0)),
                pl.BlockSpec((2, HG_WIDTH), lambda b, t: (0, 0)),
                pl.BlockSpec((1, HG_WIDTH), lambda b, t: (0, 0))]
    args = [hg3, hgrn_lb, g_hgrn.reshape(1, HG_WIDTH)]
    if has_s0:
        in_specs.append(st_spec)
        args.append(s0)
    return pl.pallas_call(
        functools.partial(_hgrn_kernel, tc=tc, has_s0=has_s0),
        grid=(nb, tl // tc),
        in_specs=in_specs,
        out_specs=[pl.BlockSpec((None, tc, HG_WIDTH), lambda b, t: (b, t, 0)), st_spec],
        out_shape=[jax.ShapeDtypeStruct((nb, tl, HG_WIDTH), BF16),
                   jax.ShapeDtypeStruct((nb, HG_HEADS, HG_DK, HG_DV), F32)],
        scratch_shapes=[pltpu.VMEM((HG_HEADS, HG_DV, HG_DK), F32)],
        compiler_params=_cparams(2),
        name="hgrn",
    )(*args)


def _merge_kernel(a_ref, b_ref, z_ref, x_ref, mod_ref, wa_ref, wb_ref, wo_ref, gpost_ref, gpre_ref,
                  wrh_ref, wrl_ref, br_ref, x1_ref, h2_ref, gate_ref, loc_ref, cnt_ref, *, tm, nseg, sub):
    i = pl.program_id(0)
    seg = tm // nseg

    def mod_rows(r, part):
        if nseg == 1:
            return mod_ref[0, r:r + 1, :]
        segs = range(part * sub // seg, (part + 1) * sub // seg)
        return jnp.concatenate([jnp.broadcast_to(mod_ref[s, r:r + 1, :], (seg, D_MODEL)) for s in segs], axis=0)

    @pl.when(i == 0)
    def _():
        cnt_ref[...] = jnp.zeros_like(cnt_ref)

    earlier = (lax.broadcasted_iota(I32, (sub, sub), 0) < lax.broadcasted_iota(I32, (sub, sub), 1)).astype(BF16)
    lower = (lax.broadcasted_iota(I32, (N_EXPERTS, N_EXPERTS), 1)
             < lax.broadcasted_iota(I32, (N_EXPERTS, N_EXPERTS), 0)).astype(BF16)
    eio = lax.broadcasted_iota(I32, (N_EXPERTS, sub), 0).astype(F32)
    tile_lane = lax.broadcasted_iota(I32, cnt_ref.shape, 1)
    for part in range(tm // sub):
        rs = slice(part * sub, (part + 1) * sub)
        za = z_ref[rs, 0:D_MODEL]
        zh = z_ref[rs, D_MODEL:2 * D_MODEL]
        m = (_sigmoid(za) * jnp.dot(a_ref[rs, :], wa_ref[...], preferred_element_type=F32)
             + _sigmoid(zh) * jnp.dot(b_ref[rs, :], wb_ref[...], preferred_element_type=F32))
        y = jnp.dot(m.astype(BF16), wo_ref[...], preferred_element_type=F32)
        x1 = x_ref[rs, :] + mod_rows(2, part) * (_rms(y) * gpost_ref[...])
        x1_ref[rs, :] = x1
        h2 = (_rms(x1) * gpre_ref[...]) * (1.0 + mod_rows(4, part)) + mod_rows(3, part)
        h_hi = h2.astype(BF16)
        h2_ref[rs, :] = h_hi

        h_lo = (h2 - h_hi.astype(F32)).astype(BF16)
        lg = (jnp.dot(h_hi, wrh_ref[...], preferred_element_type=F32)
              + jnp.dot(h_hi, wrl_ref[...], preferred_element_type=F32)
              + jnp.dot(h_lo, wrh_ref[...], preferred_element_type=F32))
        logits = lg.T[0:N_EXPERTS, :] + br_ref[...]
        vals, sels = [], []
        cur = logits
        for k in range(TOP_K):
            mx = jnp.max(cur, axis=0, keepdims=True)
            ik = jnp.min(jnp.where(cur == mx, eio, float(N_EXPERTS)), axis=0, keepdims=True)
            sel = eio == ik
            vals.append(mx)
            sels.append(sel)
            cur = jnp.where(sel, -jnp.inf, cur)
        es = [jnp.exp(v - vals[0]) for v in vals]
        den = es[0] + es[1] + es[2] + es[3]
        for k in range(TOP_K):
            gate_ref[k:k + 1, rs] = es[k] / den

        onehot = jnp.zeros(logits.shape, F32)
        for k in range(TOP_K):
            onehot = onehot + sels[k].astype(F32)
        n8 = jnp.ceil(jnp.sum(onehot, axis=1, keepdims=True) * 0.125) * 8.0
        base = (jnp.dot(lower, jnp.broadcast_to(n8, onehot.shape).astype(BF16), preferred_element_type=F32)
                + jnp.dot(onehot.astype(BF16), earlier, preferred_element_type=F32))
        for k in range(TOP_K):
            loc_ref[k:k + 1, rs] = jnp.sum(jnp.where(sels[k], base, 0.0), axis=0, keepdims=True).astype(I32)
        cnt_ref[...] = jnp.where(tile_lane == i * (tm // sub) + part, n8, cnt_ref[...])


def _merge(a2, b2, z2, x2, ada3, b_off, seq, wa, wb, wo, g_post, g_pre, wr_hi, wr_lo, br, tm, sub):
    n = x2.shape[0]
    if tm >= seq:
        nseg = tm // seq
        assert b_off % nseg == 0
        mod_idx = lambda i: (i + b_off // nseg, 0, 0)
    else:
        nseg = 1
        tpb = seq // tm
        mod_idx = lambda i: (i // tpb + b_off, 0, 0)
    row = lambda i: (i, 0)
    const = lambda i: (0, 0)
    col = lambda i: (0, i)
    return pl.pallas_call(
        functools.partial(_merge_kernel, tm=tm, nseg=nseg, sub=sub),
        grid=(n // tm,),
        in_specs=[pl.BlockSpec((tm, ATT_WIDTH), row),
                  pl.BlockSpec((tm, HG_WIDTH), row),
                  pl.BlockSpec((tm, 2 * D_MODEL), row),
                  pl.BlockSpec((tm, D_MODEL), row),
                  pl.BlockSpec((nseg, 6, D_MODEL), mod_idx),
                  pl.BlockSpec((ATT_WIDTH, D_MODEL), const),
                  pl.BlockSpec((HG_WIDTH, D_MODEL), const),
                  pl.BlockSpec((D_MODEL, D_MODEL), const),
                  pl.BlockSpec((1, D_MODEL), const),
                  pl.BlockSpec((1, D_MODEL), const),
                  pl.BlockSpec((D_MODEL, LANES), const),
                  pl.BlockSpec((D_MODEL, LANES), const),
                  pl.BlockSpec((N_EXPERTS, 1), const)],
        out_specs=[pl.BlockSpec((tm, D_MODEL), row),
                   pl.BlockSpec((tm, D_MODEL), row),
                   pl.BlockSpec((TOP_K, tm), col),
                   pl.BlockSpec((TOP_K, tm), col),
                   pl.BlockSpec((N_EXPERTS, n // sub), const)],
        out_shape=[jax.ShapeDtypeStruct((n, D_MODEL), F32),
                   jax.ShapeDtypeStruct((n, D_MODEL), BF16),
                   jax.ShapeDtypeStruct((TOP_K, n), F32),
                   jax.ShapeDtypeStruct((TOP_K, n), I32),
                   jax.ShapeDtypeStruct((N_EXPERTS, n // sub), F32)],
        compiler_params=_cparams(1),
        name="merge",
    )(a2, b2, z2, x2, ada3, wa, wb, wo, g_post.reshape(1, D_MODEL), g_pre.reshape(1, D_MODEL),
      wr_hi, wr_lo, br.reshape(N_EXPERTS, 1))


def _pow2_pieces(n, max_piece, fn):
    sz = max_piece
    while sz >= SUBLANES:
        off = n & ~(2 * sz - 1)

        @pl.when((n & sz) != 0)
        def _(off=off, sz=sz):
            fn(off, sz)

        sz //= 2


def _group_copies(meta_ref, td, make_copy, act):
    def body(e, _):
        loc0 = meta_ref[0, e]
        n8 = meta_ref[0, N_EXPERTS + e]
        glob0 = meta_ref[0, 2 * N_EXPERTS + e]
        _pow2_pieces(n8, td, lambda off, sz: act(make_copy(pl.multiple_of(loc0 + off, SUBLANES),
                                                           pl.multiple_of(glob0 + off, SUBLANES), sz)))
        return 0

    lax.fori_loop(0, N_EXPERTS, body, 0)


def _dispatch_kernel(meta_ref, mprev_ref, tail_ref, loc_ref, h_ref, buf_ref, srt_ref, zero_ref, sem, zsem,
                     *, td, blk):
    i = pl.program_id(0)
    slot = i % 2
    n_sorted = srt_ref.shape[1]

    @pl.when(i == 0)
    def _():
        zero_ref[...] = jnp.zeros_like(zero_ref)

        def zero_copy(start, off, sz):
            return pltpu.make_async_copy(zero_ref.at[pl.ds(0, sz)],
                                         buf_ref.at[pl.ds(pl.multiple_of(start + off, SUBLANES), sz)], zsem)

        for act in (lambda c: c.start(), lambda c: c.wait()):
            def body(e, _, act=act):
                start = tail_ref[0, e]
                _pow2_pieces(tail_ref[1, e], blk // 2, lambda off, sz: act(zero_copy(start, off, sz)))
                return 0

            lax.fori_loop(0, N_EXPERTS, body, 0)

            def unused(j, _, act=act):
                act(zero_copy(tail_ref[0, N_EXPERTS], j * (blk // 2), blk // 2))
                return 0

            lax.fori_loop(0, tail_ref[1, N_EXPERTS] // (blk // 2), unused, 0)

    rio = lax.broadcasted_iota(I32, (n_sorted, td), 0)
    hit = rio == loc_ref[0:1, :]
    for k in range(1, TOP_K):
        hit = hit | (rio == loc_ref[k:k + 1, :])
    srt_ref[slot] = jnp.dot(hit.astype(BF16), h_ref[...], preferred_element_type=F32)

    def copies(m_ref, s, act):
        def make_copy(loc0, glob0, sz):
            return pltpu.make_async_copy(srt_ref.at[s, pl.ds(loc0, sz)], buf_ref.at[pl.ds(glob0, sz)], sem.at[s])

        _group_copies(m_ref, td, make_copy, act)

    copies(meta_ref, slot, lambda c: c.start())

    @pl.when(i > 0)
    def _():
        copies(mprev_ref, 1 - slot, lambda c: c.wait())

    @pl.when(i == pl.num_programs(0) - 1)
    def _():
        copies(meta_ref, slot, lambda c: c.wait())


def _dispatch(h2, loc, meta, tail, n_rows, td, blk):
    n = h2.shape[0]
    n_sorted = TOP_K * td + SUBLANES * N_EXPERTS
    return pl.pallas_call(
        functools.partial(_dispatch_kernel, td=td, blk=blk),
        grid=(n // td,),
        in_specs=[pl.BlockSpec((None, 1, 3 * N_EXPERTS), lambda i: (i, 0, 0), memory_space=pltpu.SMEM),
                  pl.BlockSpec((None, 1, 3 * N_EXPERTS), lambda i: (jnp.maximum(i - 1, 0), 0, 0),
                               memory_space=pltpu.SMEM),
                  pl.BlockSpec(memory_space=pltpu.SMEM),
                  pl.BlockSpec((TOP_K, td), lambda i: (0, i)),
                  pl.BlockSpec((td, D_MODEL), lambda i: (i, 0))],
        out_specs=pl.BlockSpec(memory_space=pl.ANY),
        out_shape=jax.ShapeDtypeStruct((n_rows, D_MODEL), F32),
        scratch_shapes=[pltpu.VMEM((2, n_sorted, D_MODEL), F32), pltpu.VMEM((blk // 2, D_MODEL), F32),
                        pltpu.SemaphoreType.DMA((2,)), pltpu.SemaphoreType.DMA],
        compiler_params=_cparams(1),
        name="dispatch",
    )(meta, meta, tail, loc, h2)


def _expert_kernel(be_ref, nu_ref, x_ref, wgu_ref, bgu_ref, wd_ref, bd_ref, y_ref):
    del be_ref
    i = pl.program_id(0)

    @pl.when(i < nu_ref[0])
    def _():
        gu = jnp.dot(x_ref[...].astype(BF16), wgu_ref[...], preferred_element_type=F32) + bgu_ref[...]
        gate = jnp.minimum(gu[:, :D_FF], SWIGLU_LIMIT)
        up = jnp.clip(gu[:, D_FF:], -SWIGLU_LIMIT, SWIGLU_LIMIT)
        act = gate * _sigmoid(SWIGLU_ALPHA * gate) * (up + 1.0)
        y_ref[...] = jnp.dot(act.astype(BF16), wd_ref[...], preferred_element_type=F32) + bd_ref[...]

    @pl.when(i >= nu_ref[0])
    def _():
        y_ref[...] = jnp.zeros_like(y_ref)


def _experts(buf, blk_expert, n_used, wgu_bf, bgu, wd_bf, bd, blk):
    n_rows = buf.shape[0]
    used = lambda i, be, nu: (jnp.minimum(i, nu[0] - 1), 0)
    grid_spec = pltpu.PrefetchScalarGridSpec(
        num_scalar_prefetch=2,
        grid=(n_rows // blk,),
        in_specs=[pl.BlockSpec((blk, D_MODEL), used),
                  pl.BlockSpec((None, D_MODEL, 2 * D_FF), lambda i, be, nu: (be[i], 0, 0)),
                  pl.BlockSpec((None, 1, 2 * D_FF), lambda i, be, nu: (be[i], 0, 0)),
                  pl.BlockSpec((None, D_FF, D_MODEL), lambda i, be, nu: (be[i], 0, 0)),
                  pl.BlockSpec((None, 1, D_MODEL), lambda i, be, nu: (be[i], 0, 0))],
        out_specs=pl.BlockSpec((blk, D_MODEL), lambda i, be, nu: (i, 0)),
    )
    return pl.pallas_call(
        _expert_kernel,
        grid_spec=grid_spec,
        out_shape=jax.ShapeDtypeStruct((n_rows, D_MODEL), F32),
        compiler_params=_cparams(1),
        name="experts",
    )(blk_expert, n_used, buf, wgu_bf, bgu.reshape(N_EXPERTS, 1, 2 * D_FF), wd_bf, bd.reshape(N_EXPERTS, 1, D_MODEL))


def _combine_kernel(meta_ref, mnext_ref, loc_ref, gate_ref, x1_ref, mod_ref, gpost_ref, ybuf_ref, o_ref,
                    ys_ref, sem, *, td, nseg):
    i = pl.program_id(0)
    slot = i % 2
    n_sorted = ys_ref.shape[1]

    def fetch(m_ref, s, act):
        def make_copy(loc0, glob0, sz):
            return pltpu.make_async_copy(ybuf_ref.at[pl.ds(glob0, sz)], ys_ref.at[s, pl.ds(loc0, sz)], sem.at[s])

        _group_copies(m_ref, td, make_copy, act)

    @pl.when(i == 0)
    def _():
        ys_ref[...] = jnp.zeros_like(ys_ref)
        fetch(meta_ref, 0, lambda c: c.start())

    @pl.when(i + 1 < pl.num_programs(0))
    def _():
        fetch(mnext_ref, 1 - slot, lambda c: c.start())

    fetch(meta_ref, slot, lambda c: c.wait())
    y = ys_ref[slot].astype(BF16)
    jio = lax.broadcasted_iota(I32, (td, n_sorted), 1)
    w = jnp.zeros((td, n_sorted), F32)
    for k in range(TOP_K):
        w = w + jnp.where(jio == loc_ref[:, k:k + 1], gate_ref[:, k:k + 1], 0.0)
    w_hi = w.astype(BF16)
    w_lo = (w - w_hi.astype(F32)).astype(BF16)
    mo = jnp.dot(w_hi, y, preferred_element_type=F32) + jnp.dot(w_lo, y, preferred_element_type=F32)
    if nseg == 1:
        gate2 = mod_ref[0, 5:6, :]
    else:
        gate2 = jnp.concatenate([jnp.broadcast_to(mod_ref[s, 5:6, :], (td // nseg, D_MODEL)) for s in range(nseg)],
                                axis=0)
    o_ref[...] = x1_ref[...] + gate2 * (_rms(mo) * gpost_ref[...])


def _combine(meta, loc_t, gates_t, x1, ada3, b_off, seq, g_post, ybuf, td):
    n = x1.shape[0]
    n_tiles = n // td
    n_sorted = TOP_K * td + SUBLANES * N_EXPERTS
    if td >= seq:
        nseg = td // seq
        assert b_off % nseg == 0
        mod_idx = lambda i: (i + b_off // nseg, 0, 0)
    else:
        nseg = 1
        tpb = seq // td
        mod_idx = lambda i: (i // tpb + b_off, 0, 0)
    return pl.pallas_call(
        functools.partial(_combine_kernel, td=td, nseg=nseg),
        grid=(n_tiles,),
        in_specs=[pl.BlockSpec((None, 1, 3 * N_EXPERTS), lambda i: (i, 0, 0), memory_space=pltpu.SMEM),
                  pl.BlockSpec((None, 1, 3 * N_EXPERTS), lambda i: (jnp.minimum(i + 1, n_tiles - 1), 0, 0),
                               memory_space=pltpu.SMEM),
                  pl.BlockSpec((td, TOP_K), lambda i: (i, 0)),
                  pl.BlockSpec((td, TOP_K), lambda i: (i, 0)),
                  pl.BlockSpec((td, D_MODEL), lambda i: (i, 0)),
                  pl.BlockSpec((nseg, 6, D_MODEL), mod_idx),
                  pl.BlockSpec((1, D_MODEL), lambda i: (0, 0)),
                  pl.BlockSpec(memory_space=pl.ANY)],
        out_specs=pl.BlockSpec((td, D_MODEL), lambda i: (i, 0)),
        out_shape=jax.ShapeDtypeStruct((n, D_MODEL), F32),
        scratch_shapes=[pltpu.VMEM((2, n_sorted, D_MODEL), F32), pltpu.SemaphoreType.DMA((2,))],
        compiler_params=_cparams(1),
        name="combine",
    )(meta, meta, loc_t, gates_t, x1, ada3, g_post.reshape(1, D_MODEL), ybuf)


def _tile_sizes(nb, seq):
    n = nb * seq
    big = n >= 8192
    tmm = 512 if seq >= 512 else n
    return dict(
        tm=min(seq, 256),
        tmm=tmm,
        tq=min(seq, 256),
        tc=min(seq, 256),
        td=min(tmm, 256),
        blk=512 if big else 128,
    )


def _rope_tables(pos):
    half = HEAD_DIM // 2
    inv_freq = ROPE_THETA ** (-jnp.arange(half, dtype=F32) / half)
    ang = pos.astype(F32)[:, None] * inv_freq[None, :]
    cos, sin = jnp.cos(ang), jnp.sin(ang)
    reps = LANES // HEAD_DIM
    return (jnp.tile(jnp.concatenate([cos, cos], axis=1), (1, reps)),
            jnp.tile(jnp.concatenate([-sin, sin], axis=1), (1, reps)))


def _route_plan(cnt8, n_assign, blk):
    n_tiles = cnt8.shape[1]
    n_blocks = -(-(n_assign + n_tiles * N_EXPERTS * (SUBLANES - 1) + N_EXPERTS * (blk - 1)) // blk)
    tot = jnp.sum(cnt8, axis=1)
    padded = (tot + blk - 1) // blk * blk
    pend = jnp.cumsum(padded)
    pstart = pend - padded
    glob0 = pstart[:, None] + jnp.cumsum(cnt8, axis=1) - cnt8
    loc0 = jnp.cumsum(cnt8, axis=0) - cnt8
    meta = jnp.concatenate([loc0.T, cnt8.T, glob0.T], axis=1).reshape(n_tiles, 1, 3 * N_EXPERTS)
    blk_row = jnp.arange(n_blocks, dtype=I32) * blk
    blk_expert = jnp.minimum(jnp.sum((pend[None, :] <= blk_row[:, None]).astype(I32), axis=1), N_EXPERTS - 1)
    n_used = (pend[-1:] // blk).astype(I32)
    n_rows = n_blocks * blk
    tail = jnp.stack([jnp.concatenate([pstart + tot, pend[-1:]]),
                      jnp.concatenate([padded - tot, n_rows - pend[-1:]])])
    return meta.astype(I32), tail.astype(I32), blk_expert, n_used, n_rows


def _layer(x, ada3, b_off, pos, k_past, v_past, s0, wts):
    nb, seq, _ = x.shape
    n = nb * seq
    ts = _tile_sizes(nb, seq)
    x2 = x.reshape(n, D_MODEL)
    cos_t, sin_t = _rope_tables(pos)
    q, k, v, hg, z = _proj(x2, ada3, b_off, seq, wts['g_pre_mix'], wts['w_in'], cos_t, sin_t, ts['tm'])
    k3 = k.reshape(nb, seq, KV_WIDTH)
    v3 = v.reshape(nb, seq, KV_WIDTH)
    if k_past is None:
        koff = 0
    else:
        rows = k_past.shape[1]
        koff = rows
        k3 = jnp.concatenate([k_past.reshape(nb, rows, KV_WIDTH), k3], axis=1)
        v3 = jnp.concatenate([v_past.reshape(nb, rows, KV_WIDTH), v3], axis=1)
    y_att = _attn(q.reshape(nb, seq, ATT_WIDTH), k3, v3, wts['sinks'], ts['tq'], koff)
    y_hg, s_new = _hgrn(hg.reshape(nb, seq, 4 * HG_WIDTH), wts['hgrn_lb'], wts['g_hgrn'], s0, ts['tc'])
    td, blk = ts['td'], ts['blk']
    x1, h2, gates, loc, cnt = _merge(
        y_att.reshape(n, ATT_WIDTH), y_hg.reshape(n, HG_WIDTH), z, x2, ada3, b_off, seq,
        wts['w_br_attn'], wts['w_br_hgrn'], wts['w_out'], wts['g_post_mix'], wts['g_pre_ffn'],
        wts['w_router_hi'], wts['w_router_lo'], wts['b_router'], ts['tmm'], td)
    meta, tail, blk_expert, n_used, n_rows = _route_plan(cnt.astype(I32), n * TOP_K, blk)
    buf = _dispatch(h2, loc, meta, tail, n_rows, td, blk)
    ybuf = _experts(buf, blk_expert, n_used, wts['w_gate_up'], wts['b_gate_up'], wts['w_down'], wts['b_down'], blk)
    out = _combine(meta, loc.T, gates.T, x1, ada3, b_off, seq, wts['g_post_ffn'], ybuf, td)
    win = min(WINDOW, k3.shape[1])
    k_new = k3[:, k3.shape[1] - win:].reshape(nb, win, N_KV_HEADS, HEAD_DIM)
    v_new = v3[:, v3.shape[1] - win:].reshape(nb, win, N_KV_HEADS, HEAD_DIM)
    return out.reshape(nb, seq, D_MODEL), k_new, v_new, s_new


def kernel(x_prompt, x_sample, cache_k, cache_v, state_hgrn, c_prompt, c_sample, w_ada, b_ada, g_pre_mix, g_post_mix, g_pre_ffn, g_post_ffn, w_in, attn_sinks, hgrn_lb, g_hgrn, w_br_attn, w_br_hgrn, w_out, w_router, b_router, w_gate_up, b_gate_up, w_down, b_down):
    n_bp = x_prompt.shape[0]
    wr = jnp.pad(w_router[0], ((0, 0), (0, LANES - N_EXPERTS)))
    wr_hi = wr.astype(BF16)
    wr_lo = (wr - wr_hi.astype(F32)).astype(BF16)
    wts = dict(
        g_pre_mix=g_pre_mix[0], g_post_mix=g_post_mix[0], g_pre_ffn=g_pre_ffn[0], g_post_ffn=g_post_ffn[0],
        w_in=w_in[0].astype(BF16), sinks=attn_sinks[0], hgrn_lb=hgrn_lb, g_hgrn=g_hgrn[0],
        w_br_attn=w_br_attn[0].astype(BF16), w_br_hgrn=w_br_hgrn[0].astype(BF16), w_out=w_out[0].astype(BF16),
        w_router_hi=wr_hi, w_router_lo=wr_lo, b_router=b_router[0],
        w_gate_up=w_gate_up[0].astype(BF16), b_gate_up=b_gate_up[0],
        w_down=w_down[0].astype(BF16), b_down=b_down[0])
    ada = _ada(jnp.concatenate([c_prompt, c_sample], axis=0), w_ada[0], b_ada[0])
    ada3 = ada.reshape(ada.shape[0], 6, D_MODEL)
    pos_p = jnp.arange(x_prompt.shape[1])
    pos_s = PAST_LEN + jnp.arange(x_sample.shape[1])
    ys, ks, vs, ss = _layer(x_sample, ada3, n_bp, pos_s, cache_k[0], cache_v[0], state_hgrn[0], wts)
    yp, kp, vp, sp = _layer(x_prompt, ada3, 0, pos_p, None, None, None, wts)
    return (yp, ys, kp[None], vp[None], sp[None], ks[None], vs[None], ss[None])
```

```python
import functools

import jax
import jax.numpy as jnp
from jax import lax
from jax.experimental import pallas as pl
from jax.experimental.pallas import tpu as pltpu

F32 = jnp.float32
BF16 = jnp.bfloat16
I32 = jnp.int32

D_MODEL = 1024
PAST_LEN = 1024
CHUNK = 64
HEAD_DIM = 64
N_Q_HEADS = 8
N_KV_HEADS = 2
Q_PER_KV = N_Q_HEADS // N_KV_HEADS
ATT_WIDTH = N_Q_HEADS * HEAD_DIM
KV_WIDTH = N_KV_HEADS * HEAD_DIM
WINDOW = 128
ROPE_THETA = 10000.0
HG_HEADS = 4
HG_DK = 128
HG_DV = 128
HG_WIDTH = HG_HEADS * HG_DK
HG_BLOCK = 16
N_EXPERTS = 32
TOP_K = 4
D_FF = 1024
SWIGLU_LIMIT = 7.0
SWIGLU_ALPHA = 1.702
RMS_EPS = 1e-6
N_IN = ATT_WIDTH + 2 * KV_WIDTH + 4 * HG_WIDTH + 2 * D_MODEL
HG_OFF = ATT_WIDTH + 2 * KV_WIDTH
Z_OFF = HG_OFF + 4 * HG_WIDTH

META_LEN = 3 * N_EXPERTS + 1
LANES = 128
SUBLANES = 8
VMEM_LIMIT = 56 * 1024 * 1024

_NT = (((1,), (1,)), ((), ()))
_TN = (((0,), (0,)), ((), ()))


def _sigmoid(x):
    return 1.0 / (1.0 + jnp.exp(-x))


def _rms(x):
    return x * lax.rsqrt(jnp.mean(x * x, axis=-1, keepdims=True) + RMS_EPS)


def _cparams(n_axes):
    return pltpu.CompilerParams(dimension_semantics=("arbitrary",) * n_axes,
                                vmem_limit_bytes=VMEM_LIMIT)


def _ada_kernel(c_ref, w_ref, b_ref, o_ref):
    c = c_ref[...]
    s = c * _sigmoid(c)
    o_ref[...] = jnp.dot(s, w_ref[...], precision=lax.Precision.HIGHEST,
                         preferred_element_type=F32) + b_ref[...]


def _ada(c, w_ada, b_ada):
    nb = c.shape[0]
    n_out = w_ada.shape[1]
    tn = D_MODEL
    return pl.pallas_call(
        _ada_kernel,
        grid=(n_out // tn,),
        in_specs=[pl.BlockSpec((nb, D_MODEL), lambda j: (0, 0)),
                  pl.BlockSpec((D_MODEL, tn), lambda j: (0, j)),
                  pl.BlockSpec((1, tn), lambda j: (0, j))],
        out_specs=pl.BlockSpec((nb, tn), lambda j: (0, j)),
        out_shape=jax.ShapeDtypeStruct((nb, n_out), F32),
        compiler_params=_cparams(1),
        name="ada",
    )(c, w_ada, b_ada.reshape(1, n_out))


def _proj_kernel(x_ref, mod_ref, g_ref, w_ref, cos_ref, sin_ref,
                 q_ref, k_ref, v_ref, hg_ref, z_ref):
    x = x_ref[...]
    h = _rms(x) * g_ref[...]
    h = h * (1.0 + mod_ref[1:2, :]) + mod_ref[0:1, :]
    hb = h.astype(BF16)
    cos = cos_ref[...]
    sin = sin_ref[...]
    lane = lax.broadcasted_iota(I32, cos.shape, 1)
    first_half = (lane % HEAD_DIM) < (HEAD_DIM // 2)

    def cols(start, width):
        return jnp.dot(hb, w_ref[:, start:start + width], preferred_element_type=F32)

    def rope(p):
        partner = jnp.where(first_half, pltpu.roll(p, LANES - HEAD_DIM // 2, 1),
                            pltpu.roll(p, HEAD_DIM // 2, 1))
        return p * cos + partner * sin

    for j in range(ATT_WIDTH // LANES):
        q = rope(cols(j * LANES, LANES)) * (HEAD_DIM ** -0.5)
        q_ref[:, j * LANES:(j + 1) * LANES] = q.astype(BF16)
    k_ref[...] = rope(cols(ATT_WIDTH, KV_WIDTH))
    v_ref[...] = cols(ATT_WIDTH + KV_WIDTH, KV_WIDTH)
    for j in range(4):
        hg_ref[:, j * HG_WIDTH:(j + 1) * HG_WIDTH] = cols(HG_OFF + j * HG_WIDTH, HG_WIDTH)
    for j in range(4):
        z_ref[:, j * 512:(j + 1) * 512] = cols(Z_OFF + j * 512, 512)


def _proj(x2, ada3, b_off, seq, g_pre, w_in_bf, cos_t, sin_t, tm):
    n = x2.shape[0]
    tpb = seq // tm
    row = lambda i: (i, 0)
    return pl.pallas_call(
        _proj_kernel,
        grid=(n // tm,),
        in_specs=[pl.BlockSpec((tm, D_MODEL), row),
                  pl.BlockSpec((None, 6, D_MODEL), lambda i: (i // tpb + b_off, 0, 0)),
                  pl.BlockSpec((1, D_MODEL), lambda i: (0, 0)),
                  pl.BlockSpec((D_MODEL, N_IN), lambda i: (0, 0)),
                  pl.BlockSpec((tm, LANES), lambda i: (i % tpb, 0)),
                  pl.BlockSpec((tm, LANES), lambda i: (i % tpb, 0))],
        out_specs=[pl.BlockSpec((tm, ATT_WIDTH), row),
                   pl.BlockSpec((tm, KV_WIDTH), row),
                   pl.BlockSpec((tm, KV_WIDTH), row),
                   pl.BlockSpec((tm, 4 * HG_WIDTH), row),
                   pl.BlockSpec((tm, 2 * D_MODEL), row)],
        out_shape=[jax.ShapeDtypeStruct((n, ATT_WIDTH), BF16),
                   jax.ShapeDtypeStruct((n, KV_WIDTH), F32),
                   jax.ShapeDtypeStruct((n, KV_WIDTH), F32),
                   jax.ShapeDtypeStruct((n, 4 * HG_WIDTH), F32),
                   jax.ShapeDtypeStruct((n, 2 * D_MODEL), F32)],
        compiler_params=_cparams(1),
        name="proj",
    )(x2, ada3, g_pre.reshape(1, D_MODEL), w_in_bf, cos_t, sin_t)


def _attn_kernel(sink_ref, q_ref, k_ref, v_ref, o_ref, *, tq, koff):
    t = pl.program_id(1)
    span = WINDOW + CHUNK
    for c in range(tq // CHUNK):
        r = t * tq + c * CHUNK + koff
        ks = pl.multiple_of(jnp.maximum(r - WINDOW, 0), CHUNK)
        kf = k_ref[pl.ds(ks, span), :]
        vf = v_ref[pl.ds(ks, span), :]
        kpos = ks + lax.broadcasted_iota(I32, (1, span), 1)
        valid = kpos < r + CHUNK
        lane = lax.broadcasted_iota(I32, (span, KV_WIDTH), 1)
        top_row = lax.broadcasted_iota(I32, (2 * CHUNK, 1), 0) < CHUNK
        rows = slice(c * CHUNK, (c + 1) * CHUNK)
        for g in range(N_KV_HEADS):
            own = (lane // HEAD_DIM) == g
            k_own = jnp.where(own, kf, 0.0)
            v_own = jnp.where(own, vf, 0.0)
            k_oth = pltpu.roll(k_own, HEAD_DIM, 1)
            v_oth = pltpu.roll(v_own, HEAD_DIM, 1)
            k_par = (k_own, k_oth) if g == 0 else (k_oth, k_own)
            v_par = (v_own, v_oth) if g == 0 else (v_oth, v_own)
            qq = jnp.concatenate([q_ref[rows, (2 * g + i) * LANES:(2 * g + i + 1) * LANES] for i in range(2)], axis=0)
            acc = None
            for par in range(2):
                s = lax.dot_general(qq, k_par[par].astype(BF16), _NT, preferred_element_type=F32)
                s = jnp.where(valid, s, -jnp.inf)
                h_top = Q_PER_KV * g + par
                sink = jnp.where(top_row, sink_ref[h_top], sink_ref[h_top + 2])
                m = jnp.maximum(jnp.max(s, axis=-1, keepdims=True), sink)
                p = jnp.exp(s - m)
                p = p / (jnp.sum(p, axis=-1, keepdims=True) + jnp.exp(sink - m))
                o = jnp.dot(p.astype(BF16), v_par[par].astype(BF16), preferred_element_type=F32)
                acc = o if acc is None else acc + o
            for i in range(2):
                o_ref[rows, (2 * g + i) * LANES:(2 * g + i + 1) * LANES] = (
                    acc[i * CHUNK:(i + 1) * CHUNK].astype(BF16))


def _attn(q3, k3, v3, sinks, tq, koff):
    nb, tl, _ = q3.shape
    tk = k3.shape[1]
    return pl.pallas_call(
        functools.partial(_attn_kernel, tq=tq, koff=koff),
        grid=(nb, tl // tq),
        in_specs=[pl.BlockSpec(memory_space=pltpu.SMEM),
                  pl.BlockSpec((None, tq, ATT_WIDTH), lambda b, t: (b, t, 0)),
                  pl.BlockSpec((None, tk, KV_WIDTH), lambda b, t: (b, 0, 0)),
                  pl.BlockSpec((None, tk, KV_WIDTH), lambda b, t: (b, 0, 0))],
        out_specs=pl.BlockSpec((None, tq, ATT_WIDTH), lambda b, t: (b, t, 0)),
        out_shape=jax.ShapeDtypeStruct((nb, tl, ATT_WIDTH), BF16),
        compiler_params=_cparams(2),
        name="attn",
    )(sinks, q3, k3, v3)


def _hgrn_kernel(*refs, tc, has_s0):
    if has_s0:
        hg_ref, lbp_ref, gh_ref, s0_ref, y_ref, s_out_ref, st_ref = refs
    else:
        hg_ref, lbp_ref, gh_ref, y_ref, s_out_ref, st_ref = refs
        s0_ref = None
    t = pl.program_id(1)

    @pl.when(t == 0)
    def _():
        for h in range(HG_HEADS):
            st_ref[h] = s0_ref[h].T if has_s0 else jnp.zeros((HG_DV, HG_DK), F32)

    l0 = lbp_ref[0:1, :]
    l1 = lbp_ref[1:2, :]
    lm = jnp.maximum(l0, l1)
    e0 = jnp.exp(l0 - lm)
    lb = e0 / (e0 + jnp.exp(l1 - lm))

    hq = hg_ref[:, 0:HG_WIDTH]
    hf = hg_ref[:, HG_WIDTH:2 * HG_WIDTH]
    hi = hg_ref[:, 2 * HG_WIDTH:3 * HG_WIDTH]
    hz = hg_ref[:, 3 * HG_WIDTH:4 * HG_WIDTH]
    f = lb + (1.0 - lb) * _sigmoid(hf)
    q = hq * _sigmoid(hq)
    kk = 1.0 - f
    g = jnp.log(f)

    r16 = lax.broadcasted_iota(I32, g.shape, 0) % HG_BLOCK
    b = g
    suf = g
    for s in (1, 2, 4, 8):
        b = b + jnp.where(r16 >= s, pltpu.roll(b, s, 0), 0.0)
        suf = suf + jnp.where(r16 < HG_BLOCK - s, pltpu.roll(suf, tc - s, 0), 0.0)
    qt = (q * jnp.exp(b)).astype(BF16)
    kt = (kk * jnp.exp(-b)).astype(BF16)
    kd = (kk * jnp.exp(suf - g)).astype(BF16)
    vb = hi.astype(BF16)
    causal = (lax.broadcasted_iota(I32, (HG_BLOCK, HG_BLOCK), 0)
              >= lax.broadcasted_iota(I32, (HG_BLOCK, HG_BLOCK), 1))

    o_heads = []
    for h in range(HG_HEADS):
        cs = slice(h * HG_DK, (h + 1) * HG_DK)
        st = st_ref[h]
        o_blocks = []
        for j in range(tc // HG_BLOCK):
            rs = slice(j * HG_BLOCK, (j + 1) * HG_BLOCK)
            qt_b, kt_b, kd_b, v_b = qt[rs, cs], kt[rs, cs], kd[rs, cs], vb[rs, cs]
            a = lax.dot_general(qt_b, kt_b, _NT, preferred_element_type=F32)
            a = jnp.where(causal, a, 0.0)
            o = (jnp.dot(a.astype(BF16), v_b, preferred_element_type=F32)
                 + lax.dot_general(qt_b, st.astype(BF16), _NT, preferred_element_type=F32))
            o_blocks.append(o)
            dec = jnp.exp(b[j * HG_BLOCK + HG_BLOCK - 1:(j + 1) * HG_BLOCK, cs])
            st = st * dec + lax.dot_general(v_b, kd_b, _TN, preferred_element_type=F32)
        st_ref[h] = st
        o_h = jnp.concatenate(o_blocks, axis=0)
        zs = hz[:, cs]
        o_heads.append(_rms(o_h) * gh_ref[:, cs] * (zs * _sigmoid(zs)))
    y_ref[...] = jnp.concatenate(o_heads, axis=1).astype(BF16)

    @pl.when(t == pl.num_programs(1) - 1)
    def _():
        for h in range(HG_HEADS):
            s_out_ref[h] = st_ref[h].T


def _hgrn(hg3, hgrn_lb, g_hgrn, s0, tc):
    nb, tl, _ = hg3.shape
    has_s0 = s0 is not None
    st_spec = pl.BlockSpec((None, HG_HEADS, HG_DK, HG_DV), lambda b, t: (b, 0, 0, 0))
    in_specs = [pl.BlockSpec((None, tc, 4 * HG_WIDTH), lambda b, t: (b, t, 0)),
                pl.BlockSpec((2, HG_WIDTH), lambda b, t: (0, 0)),
                pl.BlockSpec((1, HG_WIDTH), lambda b, t: (0, 0))]
    args = [hg3, hgrn_lb, g_hgrn.reshape(1, HG_WIDTH)]
    if has_s0:
        in_specs.append(st_spec)
        args.append(s0)
    return pl.pallas_call(
        functools.partial(_hgrn_kernel, tc=tc, has_s0=has_s0),
        grid=(nb, tl // tc),
        in_specs=in_specs,
        out_specs=[pl.BlockSpec((None, tc, HG_WIDTH), lambda b, t: (b, t, 0)), st_spec],
        out_shape=[jax.ShapeDtypeStruct((nb, tl, HG_WIDTH), BF16),
                   jax.ShapeDtypeStruct((nb, HG_HEADS, HG_DK, HG_DV), F32)],
        scratch_shapes=[pltpu.VMEM((HG_HEADS, HG_DV, HG_DK), F32)],
        compiler_params=_cparams(2),
        name="hgrn",
    )(*args)


def _merge_kernel(a_ref, b_ref, z_ref, x_ref, mod_ref, wa_ref, wb_ref, wo_ref, gpost_ref, gpre_ref,
                  wrh_ref, wrl_ref, br_ref, x1_ref, h2_ref, gate_ref, loc_ref, cnt_ref, *, tm, nseg, sub):
    i = pl.program_id(0)
    seg = tm // nseg

    def mod_rows(r, part):
        if nseg == 1:
            return mod_ref[0, r:r + 1, :]
        segs = range(part * sub // seg, (part + 1) * sub // seg)
        return jnp.concatenate([jnp.broadcast_to(mod_ref[s, r:r + 1, :], (seg, D_MODEL)) for s in segs], axis=0)

    @pl.when(i == 0)
    def _():
        cnt_ref[...] = jnp.zeros_like(cnt_ref)

    earlier = (lax.broadcasted_iota(I32, (sub, sub), 0) < lax.broadcasted_iota(I32, (sub, sub), 1)).astype(BF16)
    lower = (lax.broadcasted_iota(I32, (N_EXPERTS, N_EXPERTS), 1)
             < lax.broadcasted_iota(I32, (N_EXPERTS, N_EXPERTS), 0)).astype(BF16)
    eio = lax.broadcasted_iota(I32, (N_EXPERTS, sub), 0).astype(F32)
    tile_lane = lax.broadcasted_iota(I32, cnt_ref.shape, 1)
    for part in range(tm // sub):
        rs = slice(part * sub, (part + 1) * sub)
        za = z_ref[rs, 0:D_MODEL]
        zh = z_ref[rs, D_MODEL:2 * D_MODEL]
        m = (_sigmoid(za) * jnp.dot(a_ref[rs, :], wa_ref[...], preferred_element_type=F32)
             + _sigmoid(zh) * jnp.dot(b_ref[rs, :], wb_ref[...], preferred_element_type=F32))
        y = jnp.dot(m.astype(BF16), wo_ref[...], preferred_element_type=F32)
        x1 = x_ref[rs, :] + mod_rows(2, part) * (_rms(y) * gpost_ref[...])
        x1_ref[rs, :] = x1
        h2 = (_rms(x1) * gpre_ref[...]) * (1.0 + mod_rows(4, part)) + mod_rows(3, part)
        h_hi = h2.astype(BF16)
        h2_ref[rs, :] = h_hi

        h_lo = (h2 - h_hi.astype(F32)).astype(BF16)
        lg = (jnp.dot(h_hi, wrh_ref[...], preferred_element_type=F32)
              + jnp.dot(h_hi, wrl_ref[...], preferred_element_type=F32)
              + jnp.dot(h_lo, wrh_ref[...], preferred_element_type=F32))
        logits = lg.T[0:N_EXPERTS, :] + br_ref[...]
        vals, sels = [], []
        cur = logits
        for k in range(TOP_K):
            mx = jnp.max(cur, axis=0, keepdims=True)
            ik = jnp.min(jnp.where(cur == mx, eio, float(N_EXPERTS)), axis=0, keepdims=True)
            sel = eio == ik
            vals.append(mx)
            sels.append(sel)
            cur = jnp.where(sel, -jnp.inf, cur)
        es = [jnp.exp(v - vals[0]) for v in vals]
        den = es[0] + es[1] + es[2] + es[3]
        for k in range(TOP_K):
            gate_ref[k:k + 1, rs] = es[k] / den

        onehot = jnp.zeros(logits.shape, F32)
        for k in range(TOP_K):
            onehot = onehot + sels[k].astype(F32)
        n8 = jnp.ceil(jnp.sum(onehot, axis=1, keepdims=True) * 0.125) * 8.0
        base = (jnp.dot(lower, jnp.broadcast_to(n8, onehot.shape).astype(BF16), preferred_element_type=F32)
                + jnp.dot(onehot.astype(BF16), earlier, preferred_element_type=F32))
        for k in range(TOP_K):
            loc_ref[k:k + 1, rs] = jnp.sum(jnp.where(sels[k], base, 0.0), axis=0, keepdims=True).astype(I32)
        cnt_ref[...] = jnp.where(tile_lane == i * (tm // sub) + part, n8, cnt_ref[...])


def _merge(a2, b2, z2, x2, ada3, b_off, seq, wa, wb, wo, g_post, g_pre, wr_hi, wr_lo, br, tm, sub):
    n = x2.shape[0]
    if tm >= seq:
        nseg = tm // seq
        assert b_off % nseg == 0
        mod_idx = lambda i: (i + b_off // nseg, 0, 0)
    else:
        nseg = 1
        tpb = seq // tm
        mod_idx = lambda i: (i // tpb + b_off, 0, 0)
    row = lambda i: (i, 0)
    const = lambda i: (0, 0)
    col = lambda i: (0, i)
    return pl.pallas_call(
        functools.partial(_merge_kernel, tm=tm, nseg=nseg, sub=sub),
        grid=(n // tm,),
        in_specs=[pl.BlockSpec((tm, ATT_WIDTH), row),
                  pl.BlockSpec((tm, HG_WIDTH), row),
                  pl.BlockSpec((tm, 2 * D_MODEL), row),
                  pl.BlockSpec((tm, D_MODEL), row),
                  pl.BlockSpec((nseg, 6, D_MODEL), mod_idx),
                  pl.BlockSpec((ATT_WIDTH, D_MODEL), const),
                  pl.BlockSpec((HG_WIDTH, D_MODEL), const),
                  pl.BlockSpec((D_MODEL, D_MODEL), const),
                  pl.BlockSpec((1, D_MODEL), const),
                  pl.BlockSpec((1, D_MODEL), const),
                  pl.BlockSpec((D_MODEL, LANES), const),
                  pl.BlockSpec((D_MODEL, LANES), const),
                  pl.BlockSpec((N_EXPERTS, 1), const)],
        out_specs=[pl.BlockSpec((tm, D_MODEL), row),
                   pl.BlockSpec((tm, D_MODEL), row),
                   pl.BlockSpec((TOP_K, tm), col),
                   pl.BlockSpec((TOP_K, tm), col),
                   pl.BlockSpec((N_EXPERTS, n // sub), const)],
        out_shape=[jax.ShapeDtypeStruct((n, D_MODEL), F32),
                   jax.ShapeDtypeStruct((n, D_MODEL), BF16),
                   jax.ShapeDtypeStruct((TOP_K, n), F32),
                   jax.ShapeDtypeStruct((TOP_K, n), I32),
                   jax.ShapeDtypeStruct((N_EXPERTS, n // sub), F32)],
        compiler_params=_cparams(1),
        name="merge",
    )(a2, b2, z2, x2, ada3, wa, wb, wo, g_post.reshape(1, D_MODEL), g_pre.reshape(1, D_MODEL),
      wr_hi, wr_lo, br.reshape(N_EXPERTS, 1))


def _pow2_pieces(n, max_piece, fn):
    sz = max_piece
    while sz >= SUBLANES:
        off = n & ~(2 * sz - 1)

        @pl.when((n & sz) != 0)
        def _(off=off, sz=sz):
            fn(off, sz)

        sz //= 2


def _start_groups(meta_ref, td, make_copy):
    def body(e, _):
        loc0 = meta_ref[0, e]
        n8 = meta_ref[0, N_EXPERTS + e]
        glob0 = meta_ref[0, 2 * N_EXPERTS + e]
        _pow2_pieces(n8, td, lambda off, sz: make_copy(pl.multiple_of(loc0 + off, SUBLANES),
                                                       pl.multiple_of(glob0 + off, SUBLANES), sz).start())
        return 0

    lax.fori_loop(0, N_EXPERTS, body, 0, unroll=2)


def _wait_groups(meta_ref, n_sorted, make_wait):
    max_piece = 1 << (n_sorted.bit_length() - 1)
    _pow2_pieces(meta_ref[0, 3 * N_EXPERTS], max_piece, lambda off, sz: make_wait(sz).wait())


def _dispatch_kernel(meta_ref, mprev_ref, tail_ref, loc_ref, h_ref, buf_ref, srt_ref, zero_ref, sem, zsem,
                     *, td, blk):
    i = pl.program_id(0)
    slot = i % 2
    n_sorted = srt_ref.shape[1]

    @pl.when(i == 0)
    def _():
        zero_ref[...] = jnp.zeros_like(zero_ref)

        def zero_copy(start, off, sz):
            return pltpu.make_async_copy(zero_ref.at[pl.ds(0, sz)],
                                         buf_ref.at[pl.ds(pl.multiple_of(start + off, SUBLANES), sz)], zsem)

        for act in (lambda c: c.start(), lambda c: c.wait()):
            def body(e, _, act=act):
                start = tail_ref[0, e]
                _pow2_pieces(tail_ref[1, e], blk // 2, lambda off, sz: act(zero_copy(start, off, sz)))
                return 0

            lax.fori_loop(0, N_EXPERTS, body, 0)

            def unused(j, _, act=act):
                act(zero_copy(tail_ref[0, N_EXPERTS], j * (blk // 2), blk // 2))
                return 0

            lax.fori_loop(0, tail_ref[1, N_EXPERTS] // (blk // 2), unused, 0)

    rio = lax.broadcasted_iota(I32, (n_sorted, td), 0)
    hit = rio == loc_ref[0:1, :]
    for k in range(1, TOP_K):
        hit = hit | (rio == loc_ref[k:k + 1, :])
    srt_ref[slot] = jnp.dot(hit.astype(BF16), h_ref[...], preferred_element_type=F32)

    def make_copy(s):
        return lambda loc0, glob0, sz: pltpu.make_async_copy(
            srt_ref.at[s, pl.ds(loc0, sz)], buf_ref.at[pl.ds(glob0, sz)], sem.at[s])

    _start_groups(meta_ref, td, make_copy(slot))

    @pl.when(i > 0)
    def _():
        _wait_groups(mprev_ref, n_sorted, lambda sz: make_copy(1 - slot)(0, 0, sz))

    @pl.when(i == pl.num_programs(0) - 1)
    def _():
        _wait_groups(meta_ref, n_sorted, lambda sz: make_copy(slot)(0, 0, sz))


def _dispatch(h2, loc, meta, tail, n_rows, td, blk):
    n = h2.shape[0]
    n_sorted = TOP_K * td + SUBLANES * N_EXPERTS
    return pl.pallas_call(
        functools.partial(_dispatch_kernel, td=td, blk=blk),
        grid=(n // td,),
        in_specs=[pl.BlockSpec((None, 1, META_LEN), lambda i: (i, 0, 0), memory_space=pltpu.SMEM),
                  pl.BlockSpec((None, 1, META_LEN), lambda i: (jnp.maximum(i - 1, 0), 0, 0),
                               memory_space=pltpu.SMEM),
                  pl.BlockSpec(memory_space=pltpu.SMEM),
                  pl.BlockSpec((TOP_K, td), lambda i: (0, i)),
                  pl.BlockSpec((td, D_MODEL), lambda i: (i, 0))],
        out_specs=pl.BlockSpec(memory_space=pl.ANY),
        out_shape=jax.ShapeDtypeStruct((n_rows, D_MODEL), F32),
        scratch_shapes=[pltpu.VMEM((2, n_sorted, D_MODEL), F32), pltpu.VMEM((blk // 2, D_MODEL), F32),
                        pltpu.SemaphoreType.DMA((2,)), pltpu.SemaphoreType.DMA],
        compiler_params=_cparams(1),
        name="dispatch",
    )(meta, meta, tail, loc, h2)


def _expert_kernel(be_ref, nu_ref, x_ref, wgu_ref, bgu_ref, wd_ref, bd_ref, y_ref, wgu_bf, wd_bf):
    i = pl.program_id(0)

    @pl.when(i < nu_ref[0])
    def _():
        @pl.when((i == 0) | (be_ref[i] != be_ref[jnp.maximum(i - 1, 0)]))
        def _():
            wgu_bf[...] = wgu_ref[...].astype(BF16)
            wd_bf[...] = wd_ref[...].astype(BF16)

        gu = jnp.dot(x_ref[...].astype(BF16), wgu_bf[...], preferred_element_type=F32) + bgu_ref[...]
        gate = jnp.minimum(gu[:, :D_FF], SWIGLU_LIMIT)
        up = jnp.clip(gu[:, D_FF:], -SWIGLU_LIMIT, SWIGLU_LIMIT)
        act = gate * _sigmoid(SWIGLU_ALPHA * gate) * (up + 1.0)
        y_ref[...] = jnp.dot(act.astype(BF16), wd_bf[...], preferred_element_type=F32) + bd_ref[...]

    @pl.when(i >= nu_ref[0])
    def _():
        y_ref[...] = jnp.zeros_like(y_ref)


def _experts(buf, blk_expert, n_used, wgu, bgu, wd, bd, blk):
    n_rows = buf.shape[0]
    used = lambda i, be, nu: (jnp.minimum(i, nu[0] - 1), 0)
    grid_spec = pltpu.PrefetchScalarGridSpec(
        num_scalar_prefetch=2,
        grid=(n_rows // blk,),
        in_specs=[pl.BlockSpec((blk, D_MODEL), used),
                  pl.BlockSpec((None, D_MODEL, 2 * D_FF), lambda i, be, nu: (be[i], 0, 0)),
                  pl.BlockSpec((None, 1, 2 * D_FF), lambda i, be, nu: (be[i], 0, 0)),
                  pl.BlockSpec((None, D_FF, D_MODEL), lambda i, be, nu: (be[i], 0, 0)),
                  pl.BlockSpec((None, 1, D_MODEL), lambda i, be, nu: (be[i], 0, 0))],
        out_specs=pl.BlockSpec((blk, D_MODEL), lambda i, be, nu: (i, 0)),
        scratch_shapes=[pltpu.VMEM((D_MODEL, 2 * D_FF), BF16), pltpu.VMEM((D_FF, D_MODEL), BF16)],
    )
    return pl.pallas_call(
        _expert_kernel,
        grid_spec=grid_spec,
        out_shape=jax.ShapeDtypeStruct((n_rows, D_MODEL), F32),
        compiler_params=_cparams(1),
        name="experts",
    )(blk_expert, n_used, buf, wgu, bgu.reshape(N_EXPERTS, 1, 2 * D_FF), wd, bd.reshape(N_EXPERTS, 1, D_MODEL))


def _combine_kernel(meta_ref, mnext_ref, loc_ref, gate_ref, x1_ref, mod_ref, gpost_ref, ybuf_ref, o_ref,
                    ys_ref, sem, *, td, nseg):
    i = pl.program_id(0)
    slot = i % 2
    n_sorted = ys_ref.shape[1]

    def make_copy(s):
        return lambda loc0, glob0, sz: pltpu.make_async_copy(
            ybuf_ref.at[pl.ds(glob0, sz)], ys_ref.at[s, pl.ds(loc0, sz)], sem.at[s])

    @pl.when(i == 0)
    def _():
        ys_ref[...] = jnp.zeros_like(ys_ref)
        _start_groups(meta_ref, td, make_copy(0))

    @pl.when(i + 1 < pl.num_programs(0))
    def _():
        _start_groups(mnext_ref, td, make_copy(1 - slot))

    _wait_groups(meta_ref, n_sorted, lambda sz: make_copy(slot)(0, 0, sz))
    y = ys_ref[slot].astype(BF16)
    jio = lax.broadcasted_iota(I32, (td, n_sorted), 1)
    w = jnp.zeros((td, n_sorted), F32)
    for k in range(TOP_K):
        w = w + jnp.where(jio == loc_ref[:, k:k + 1], gate_ref[:, k:k + 1], 0.0)
    w_hi = w.astype(BF16)
    w_lo = (w - w_hi.astype(F32)).astype(BF16)
    mo = jnp.dot(w_hi, y, preferred_element_type=F32) + jnp.dot(w_lo, y, preferred_element_type=F32)
    if nseg == 1:
        gate2 = mod_ref[0, 5:6, :]
    else:
        gate2 = jnp.concatenate([jnp.broadcast_to(mod_ref[s, 5:6, :], (td // nseg, D_MODEL)) for s in range(nseg)],
                                axis=0)
    o_ref[...] = x1_ref[...] + gate2 * (_rms(mo) * gpost_ref[...])


def _combine(meta, loc_t, gates_t, x1, ada3, b_off, seq, g_post, ybuf, td):
    n = x1.shape[0]
    n_tiles = n // td
    n_sorted = TOP_K * td + SUBLANES * N_EXPERTS
    if td >= seq:
        nseg = td // seq
        assert b_off % nseg == 0
        mod_idx = lambda i: (i + b_off // nseg, 0, 0)
    else:
        nseg = 1
        tpb = seq // td
        mod_idx = lambda i: (i // tpb + b_off, 0, 0)
    return pl.pallas_call(
        functools.partial(_combine_kernel, td=td, nseg=nseg),
        grid=(n_tiles,),
        in_specs=[pl.BlockSpec((None, 1, META_LEN), lambda i: (i, 0, 0), memory_space=pltpu.SMEM),
                  pl.BlockSpec((None, 1, META_LEN), lambda i: (jnp.minimum(i + 1, n_tiles - 1), 0, 0),
                               memory_space=pltpu.SMEM),
                  pl.BlockSpec((td, TOP_K), lambda i: (i, 0)),
                  pl.BlockSpec((td, TOP_K), lambda i: (i, 0)),
                  pl.BlockSpec((td, D_MODEL), lambda i: (i, 0)),
                  pl.BlockSpec((nseg, 6, D_MODEL), mod_idx),
                  pl.BlockSpec((1, D_MODEL), lambda i: (0, 0)),
                  pl.BlockSpec(memory_space=pl.ANY)],
        out_specs=pl.BlockSpec((td, D_MODEL), lambda i: (i, 0)),
        out_shape=jax.ShapeDtypeStruct((n, D_MODEL), F32),
        scratch_shapes=[pltpu.VMEM((2, n_sorted, D_MODEL), F32), pltpu.SemaphoreType.DMA((2,))],
        compiler_params=_cparams(1),
        name="combine",
    )(meta, meta, loc_t, gates_t, x1, ada3, g_post.reshape(1, D_MODEL), ybuf)


def _tile_sizes(nb, seq):
    n = nb * seq
    big = n >= 8192
    tmm = 512 if seq >= 512 else n
    return dict(
        tm=min(seq, 256),
        tmm=tmm,
        tq=min(seq, 256),
        tc=min(seq, 512),
        td=min(tmm, 256),
        blk=512 if big else 128,
    )


def _rope_tables(pos):
    half = HEAD_DIM // 2
    inv_freq = ROPE_THETA ** (-jnp.arange(half, dtype=F32) / half)
    ang = pos.astype(F32)[:, None] * inv_freq[None, :]
    cos, sin = jnp.cos(ang), jnp.sin(ang)
    reps = LANES // HEAD_DIM
    return (jnp.tile(jnp.concatenate([cos, cos], axis=1), (1, reps)),
            jnp.tile(jnp.concatenate([-sin, sin], axis=1), (1, reps)))


def _route_plan(cnt8, n_assign, blk):
    n_tiles = cnt8.shape[1]
    n_blocks = -(-(n_assign + n_tiles * N_EXPERTS * (SUBLANES - 1) + N_EXPERTS * (blk - 1)) // blk)
    tot = jnp.sum(cnt8, axis=1)
    padded = (tot + blk - 1) // blk * blk
    pend = jnp.cumsum(padded)
    pstart = pend - padded
    glob0 = pstart[:, None] + jnp.cumsum(cnt8, axis=1) - cnt8
    loc0 = jnp.cumsum(cnt8, axis=0) - cnt8
    meta = jnp.concatenate([loc0.T, cnt8.T, glob0.T, jnp.sum(cnt8, axis=0)[:, None]],
                           axis=1).reshape(n_tiles, 1, META_LEN)
    blk_row = jnp.arange(n_blocks, dtype=I32) * blk
    blk_expert = jnp.minimum(jnp.sum((pend[None, :] <= blk_row[:, None]).astype(I32), axis=1), N_EXPERTS - 1)
    n_used = (pend[-1:] // blk).astype(I32)
    n_rows = n_blocks * blk
    tail = jnp.stack([jnp.concatenate([pstart + tot, pend[-1:]]),
                      jnp.concatenate([padded - tot, n_rows - pend[-1:]])])
    return meta.astype(I32), tail.astype(I32), blk_expert, n_used, n_rows


def _layer(x, ada3, b_off, pos, k_past, v_past, s0, wts):
    nb, seq, _ = x.shape
    n = nb * seq
    ts = _tile_sizes(nb, seq)
    x2 = x.reshape(n, D_MODEL)
    cos_t, sin_t = _rope_tables(pos)
    q, k, v, hg, z = _proj(x2, ada3, b_off, seq, wts['g_pre_mix'], wts['w_in'], cos_t, sin_t, ts['tm'])
    k3 = k.reshape(nb, seq, KV_WIDTH)
    v3 = v.reshape(nb, seq, KV_WIDTH)
    if k_past is None:
        koff = 0
    else:
        rows = k_past.shape[1]
        koff = rows
        k3 = jnp.concatenate([k_past.reshape(nb, rows, KV_WIDTH), k3], axis=1)
        v3 = jnp.concatenate([v_past.reshape(nb, rows, KV_WIDTH), v3], axis=1)
    y_att = _attn(q.reshape(nb, seq, ATT_WIDTH), k3, v3, wts['sinks'], ts['tq'], koff)
    y_hg, s_new = _hgrn(hg.reshape(nb, seq, 4 * HG_WIDTH), wts['hgrn_lb'], wts['g_hgrn'], s0, ts['tc'])
    td, blk = ts['td'], ts['blk']
    x1, h2, gates, loc, cnt = _merge(
        y_att.reshape(n, ATT_WIDTH), y_hg.reshape(n, HG_WIDTH), z, x2, ada3, b_off, seq,
        wts['w_br_attn'], wts['w_br_hgrn'], wts['w_out'], wts['g_post_mix'], wts['g_pre_ffn'],
        wts['w_router_hi'], wts['w_router_lo'], wts['b_router'], ts['tmm'], td)
    meta, tail, blk_expert, n_used, n_rows = _route_plan(cnt.astype(I32), n * TOP_K, blk)
    buf = _dispatch(h2, loc, meta, tail, n_rows, td, blk)
    ybuf = _experts(buf, blk_expert, n_used, wts['w_gate_up'], wts['b_gate_up'], wts['w_down'], wts['b_down'], blk)
    out = _combine(meta, loc.T, gates.T, x1, ada3, b_off, seq, wts['g_post_ffn'], ybuf, td)
    win = min(WINDOW, k3.shape[1])
    k_new = k3[:, k3.shape[1] - win:].reshape(nb, win, N_KV_HEADS, HEAD_DIM)
    v_new = v3[:, v3.shape[1] - win:].reshape(nb, win, N_KV_HEADS, HEAD_DIM)
    return out.reshape(nb, seq, D_MODEL), k_new, v_new, s_new


def kernel(x_prompt, x_sample, cache_k, cache_v, state_hgrn, c_prompt, c_sample, w_ada, b_ada, g_pre_mix, g_post_mix, g_pre_ffn, g_post_ffn, w_in, attn_sinks, hgrn_lb, g_hgrn, w_br_attn, w_br_hgrn, w_out, w_router, b_router, w_gate_up, b_gate_up, w_down, b_down):
    n_bp = x_prompt.shape[0]
    wr = jnp.pad(w_router[0], ((0, 0), (0, LANES - N_EXPERTS)))
    wr_hi = wr.astype(BF16)
    wr_lo = (wr - wr_hi.astype(F32)).astype(BF16)
    wts = dict(
        g_pre_mix=g_pre_mix[0], g_post_mix=g_post_mix[0], g_pre_ffn=g_pre_ffn[0], g_post_ffn=g_post_ffn[0],
        w_in=w_in[0].astype(BF16), sinks=attn_sinks[0], hgrn_lb=hgrn_lb, g_hgrn=g_hgrn[0],
        w_br_attn=w_br_attn[0].astype(BF16), w_br_hgrn=w_br_hgrn[0].astype(BF16), w_out=w_out[0].astype(BF16),
        w_router_hi=wr_hi, w_router_lo=wr_lo, b_router=b_router[0],
        w_gate_up=w_gate_up[0], b_gate_up=b_gate_up[0], w_down=w_down[0], b_down=b_down[0])
    ada = _ada(jnp.concatenate([c_prompt, c_sample], axis=0), w_ada[0], b_ada[0])
    ada3 = ada.reshape(ada.shape[0], 6, D_MODEL)
    pos_p = jnp.arange(x_prompt.shape[1])
    pos_s = PAST_LEN + jnp.arange(x_sample.shape[1])
    ys, ks, vs, ss = _layer(x_sample, ada3, n_bp, pos_s, cache_k[0], cache_v[0], state_hgrn[0], wts)
    yp, kp, vp, sp = _layer(x_prompt, ada3, 0, pos_p, None, None, None, wts)
    return (yp, ys, kp[None], vp[None], sp[None], ks[None], vs[None], ss[None])
```

```python
import functools

import jax
import jax.numpy as jnp
from jax import lax
from jax.experimental import pallas as pl
from jax.experimental.pallas import tpu as pltpu

F32 = jnp.float32
BF16 = jnp.bfloat16
I32 = jnp.int32

D_MODEL = 1024
PAST_LEN = 1024
CHUNK = 64
HEAD_DIM = 64
N_Q_HEADS = 8
N_KV_HEADS = 2
Q_PER_KV = N_Q_HEADS // N_KV_HEADS
ATT_WIDTH = N_Q_HEADS * HEAD_DIM
KV_WIDTH = N_KV_HEADS * HEAD_DIM
WINDOW = 128
ROPE_THETA = 10000.0
HG_HEADS = 4
HG_DK = 128
HG_DV = 128
HG_WIDTH = HG_HEADS * HG_DK
HG_BLOCK = 16
N_EXPERTS = 32
TOP_K = 4
D_FF = 1024
SWIGLU_LIMIT = 7.0
SWIGLU_ALPHA = 1.702
RMS_EPS = 1e-6
N_IN = ATT_WIDTH + 2 * KV_WIDTH + 4 * HG_WIDTH + 2 * D_MODEL
HG_OFF = ATT_WIDTH + 2 * KV_WIDTH
Z_OFF = HG_OFF + 4 * HG_WIDTH

META_LEN = 3 * N_EXPERTS + 1
LANES = 128
SUBLANES = 8
VMEM_LIMIT = 56 * 1024 * 1024

_NT = (((1,), (1,)), ((), ()))
_TN = (((0,), (0,)), ((), ()))


def _sigmoid(x):
    return 1.0 / (1.0 + jnp.exp(-x))


def _rms(x):
    return x * lax.rsqrt(jnp.mean(x * x, axis=-1, keepdims=True) + RMS_EPS)


def _cparams(n_axes):
    return pltpu.CompilerParams(dimension_semantics=("arbitrary",) * n_axes,
                                vmem_limit_bytes=VMEM_LIMIT)


def _ada_kernel(c_ref, w_ref, b_ref, o_ref):
    c = c_ref[...]
    s = c * _sigmoid(c)
    o_ref[...] = jnp.dot(s, w_ref[...], precision=lax.Precision.HIGHEST,
                         preferred_element_type=F32) + b_ref[...]


def _ada(c, w_ada, b_ada):
    nb = c.shape[0]
    n_out = w_ada.shape[1]
    tn = D_MODEL
    return pl.pallas_call(
        _ada_kernel,
        grid=(n_out // tn,),
        in_specs=[pl.BlockSpec((nb, D_MODEL), lambda j: (0, 0)),
                  pl.BlockSpec((D_MODEL, tn), lambda j: (0, j)),
                  pl.BlockSpec((1, tn), lambda j: (0, j))],
        out_specs=pl.BlockSpec((nb, tn), lambda j: (0, j)),
        out_shape=jax.ShapeDtypeStruct((nb, n_out), F32),
        compiler_params=_cparams(1),
        name="ada",
    )(c, w_ada, b_ada.reshape(1, n_out))


def _proj_kernel(x_ref, mod_ref, g_ref, w_ref, cos_ref, sin_ref,
                 q_ref, k_ref, v_ref, hg_ref, z_ref):
    x = x_ref[...]
    h = _rms(x) * g_ref[...]
    h = h * (1.0 + mod_ref[1:2, :]) + mod_ref[0:1, :]
    hb = h.astype(BF16)
    cos = cos_ref[...]
    sin = sin_ref[...]
    lane = lax.broadcasted_iota(I32, cos.shape, 1)
    first_half = (lane % HEAD_DIM) < (HEAD_DIM // 2)

    def cols(start, width):
        return jnp.dot(hb, w_ref[:, start:start + width], preferred_element_type=F32)

    def rope(p):
        partner = jnp.where(first_half, pltpu.roll(p, LANES - HEAD_DIM // 2, 1),
                            pltpu.roll(p, HEAD_DIM // 2, 1))
        return p * cos + partner * sin

    for j in range(ATT_WIDTH // LANES):
        q = rope(cols(j * LANES, LANES)) * (HEAD_DIM ** -0.5)
        q_ref[:, j * LANES:(j + 1) * LANES] = q.astype(BF16)
    k_ref[...] = rope(cols(ATT_WIDTH, KV_WIDTH))
    v_ref[...] = cols(ATT_WIDTH + KV_WIDTH, KV_WIDTH)
    for j in range(4):
        hg_ref[:, j * HG_WIDTH:(j + 1) * HG_WIDTH] = cols(HG_OFF + j * HG_WIDTH, HG_WIDTH)
    for j in range(4):
        z_ref[:, j * 512:(j + 1) * 512] = cols(Z_OFF + j * 512, 512)


def _proj(x2, ada3, b_off, seq, g_pre, w_in_bf, cos_t, sin_t, tm):
    n = x2.shape[0]
    tpb = seq // tm
    row = lambda i: (i, 0)
    return pl.pallas_call(
        _proj_kernel,
        grid=(n // tm,),
        in_specs=[pl.BlockSpec((tm, D_MODEL), row),
                  pl.BlockSpec((None, 6, D_MODEL), lambda i: (i // tpb + b_off, 0, 0)),
                  pl.BlockSpec((1, D_MODEL), lambda i: (0, 0)),
                  pl.BlockSpec((D_MODEL, N_IN), lambda i: (0, 0)),
                  pl.BlockSpec((tm, LANES), lambda i: (i % tpb, 0)),
                  pl.BlockSpec((tm, LANES), lambda i: (i % tpb, 0))],
        out_specs=[pl.BlockSpec((tm, ATT_WIDTH), row),
                   pl.BlockSpec((tm, KV_WIDTH), row),
                   pl.BlockSpec((tm, KV_WIDTH), row),
                   pl.BlockSpec((tm, 4 * HG_WIDTH), row),
                   pl.BlockSpec((tm, 2 * D_MODEL), row)],
        out_shape=[jax.ShapeDtypeStruct((n, ATT_WIDTH), BF16),
                   jax.ShapeDtypeStruct((n, KV_WIDTH), F32),
                   jax.ShapeDtypeStruct((n, KV_WIDTH), F32),
                   jax.ShapeDtypeStruct((n, 4 * HG_WIDTH), F32),
                   jax.ShapeDtypeStruct((n, 2 * D_MODEL), F32)],
        compiler_params=_cparams(1),
        name="proj",
    )(x2, ada3, g_pre.reshape(1, D_MODEL), w_in_bf, cos_t, sin_t)


def _attn_kernel(sink_ref, q_ref, k_ref, v_ref, o_ref, *, tq, koff):
    t = pl.program_id(1)
    span = WINDOW + CHUNK
    for c in range(tq // CHUNK):
        r = t * tq + c * CHUNK + koff
        ks = pl.multiple_of(jnp.maximum(r - WINDOW, 0), CHUNK)
        kf = k_ref[pl.ds(ks, span), :]
        vf = v_ref[pl.ds(ks, span), :]
        kpos = ks + lax.broadcasted_iota(I32, (1, span), 1)
        valid = kpos < r + CHUNK
        lane = lax.broadcasted_iota(I32, (span, KV_WIDTH), 1)
        top_row = lax.broadcasted_iota(I32, (2 * CHUNK, 1), 0) < CHUNK
        rows = slice(c * CHUNK, (c + 1) * CHUNK)
        for g in range(N_KV_HEADS):
            own = (lane // HEAD_DIM) == g
            k_own = jnp.where(own, kf, 0.0)
            v_own = jnp.where(own, vf, 0.0)
            k_oth = pltpu.roll(k_own, HEAD_DIM, 1)
            v_oth = pltpu.roll(v_own, HEAD_DIM, 1)
            k_par = (k_own, k_oth) if g == 0 else (k_oth, k_own)
            v_par = (v_own, v_oth) if g == 0 else (v_oth, v_own)
            qq = jnp.concatenate([q_ref[rows, (2 * g + i) * LANES:(2 * g + i + 1) * LANES] for i in range(2)], axis=0)
            acc = None
            for par in range(2):
                s = lax.dot_general(qq, k_par[par].astype(BF16), _NT, preferred_element_type=F32)
                s = jnp.where(valid, s, -jnp.inf)
                h_top = Q_PER_KV * g + par
                sink = jnp.where(top_row, sink_ref[h_top], sink_ref[h_top + 2])
                m = jnp.maximum(jnp.max(s, axis=-1, keepdims=True), sink)
                p = jnp.exp(s - m)
                p = p / (jnp.sum(p, axis=-1, keepdims=True) + jnp.exp(sink - m))
                o = jnp.dot(p.astype(BF16), v_par[par].astype(BF16), preferred_element_type=F32)
                acc = o if acc is None else acc + o
            for i in range(2):
                o_ref[rows, (2 * g + i) * LANES:(2 * g + i + 1) * LANES] = (
                    acc[i * CHUNK:(i + 1) * CHUNK].astype(BF16))


def _attn(q3, k3, v3, sinks, tq, koff):
    nb, tl, _ = q3.shape
    tk = k3.shape[1]
    return pl.pallas_call(
        functools.partial(_attn_kernel, tq=tq, koff=koff),
        grid=(nb, tl // tq),
        in_specs=[pl.BlockSpec(memory_space=pltpu.SMEM),
                  pl.BlockSpec((None, tq, ATT_WIDTH), lambda b, t: (b, t, 0)),
                  pl.BlockSpec((None, tk, KV_WIDTH), lambda b, t: (b, 0, 0)),
                  pl.BlockSpec((None, tk, KV_WIDTH), lambda b, t: (b, 0, 0))],
        out_specs=pl.BlockSpec((None, tq, ATT_WIDTH), lambda b, t: (b, t, 0)),
        out_shape=jax.ShapeDtypeStruct((nb, tl, ATT_WIDTH), BF16),
        compiler_params=_cparams(2),
        name="attn",
    )(sinks, q3, k3, v3)


def _hgrn_kernel(*refs, tc, has_s0):
    if has_s0:
        hg_ref, lbp_ref, gh_ref, s0_ref, y_ref, s_out_ref, st_ref = refs
    else:
        hg_ref, lbp_ref, gh_ref, y_ref, s_out_ref, st_ref = refs
        s0_ref = None
    t = pl.program_id(1)

    @pl.when(t == 0)
    def _():
        for h in range(HG_HEADS):
            st_ref[h] = s0_ref[h].T if has_s0 else jnp.zeros((HG_DV, HG_DK), F32)

    l0 = lbp_ref[0:1, :]
    l1 = lbp_ref[1:2, :]
    lm = jnp.maximum(l0, l1)
    e0 = jnp.exp(l0 - lm)
    lb = e0 / (e0 + jnp.exp(l1 - lm))

    hq = hg_ref[:, 0:HG_WIDTH]
    hf = hg_ref[:, HG_WIDTH:2 * HG_WIDTH]
    hi = hg_ref[:, 2 * HG_WIDTH:3 * HG_WIDTH]
    hz = hg_ref[:, 3 * HG_WIDTH:4 * HG_WIDTH]
    f = lb + (1.0 - lb) * _sigmoid(hf)
    q = hq * _sigmoid(hq)
    kk = 1.0 - f
    g = jnp.log(f)

    r16 = lax.broadcasted_iota(I32, g.shape, 0) % HG_BLOCK
    b = g
    suf = g
    for s in (1, 2, 4, 8):
        b = b + jnp.where(r16 >= s, pltpu.roll(b, s, 0), 0.0)
        suf = suf + jnp.where(r16 < HG_BLOCK - s, pltpu.roll(suf, tc - s, 0), 0.0)
    qt = (q * jnp.exp(b)).astype(BF16)
    kt = (kk * jnp.exp(-b)).astype(BF16)
    kd = (kk * jnp.exp(suf - g)).astype(BF16)
    vb = hi.astype(BF16)
    grp = min(tc, LANES)
    gi = lax.broadcasted_iota(I32, (grp, grp), 0)
    gj = lax.broadcasted_iota(I32, (grp, grp), 1)
    keep = ((gi // HG_BLOCK) == (gj // HG_BLOCK)) & (gi >= gj)
    heads = [slice(h * HG_DK, (h + 1) * HG_DK) for h in range(HG_HEADS)]
    o_groups = [[] for _ in heads]
    for g in range(tc // grp):
        gs = slice(g * grp, (g + 1) * grp)
        o_intra = []
        for cs in heads:
            a = lax.dot_general(qt[gs, cs], kt[gs, cs], _NT, preferred_element_type=F32)
            a = jnp.where(keep, a, 0.0)
            o_intra.append(jnp.dot(a.astype(BF16), vb[gs, cs], preferred_element_type=F32))
        o_inter = [[] for _ in heads]
        for j in range(grp // HG_BLOCK):
            r0 = g * grp + j * HG_BLOCK
            rs = slice(r0, r0 + HG_BLOCK)
            for h, cs in enumerate(heads):
                st = st_ref[h]
                o_inter[h].append(lax.dot_general(qt[rs, cs], st.astype(BF16), _NT, preferred_element_type=F32))
                dec = jnp.exp(b[r0 + HG_BLOCK - 1:r0 + HG_BLOCK, cs])
                st_ref[h] = st * dec + lax.dot_general(vb[rs, cs], kd[rs, cs], _TN, preferred_element_type=F32)
        for h in range(HG_HEADS):
            o_groups[h].append(o_intra[h] + jnp.concatenate(o_inter[h], axis=0))
    o_heads = []
    for h, cs in enumerate(heads):
        o_h = jnp.concatenate(o_groups[h], axis=0)
        zs = hz[:, cs]
        o_heads.append(_rms(o_h) * gh_ref[:, cs] * (zs * _sigmoid(zs)))
    y_ref[...] = jnp.concatenate(o_heads, axis=1).astype(BF16)

    @pl.when(t == pl.num_programs(1) - 1)
    def _():
        for h in range(HG_HEADS):
            s_out_ref[h] = st_ref[h].T


def _hgrn(hg3, hgrn_lb, g_hgrn, s0, tc):
    nb, tl, _ = hg3.shape
    has_s0 = s0 is not None
    st_spec = pl.BlockSpec((None, HG_HEADS, HG_DK, HG_DV), lambda b, t: (b, 0, 0, 0))
    in_specs = [pl.BlockSpec((None, tc, 4 * HG_WIDTH), lambda b, t: (b, t, 0)),
                pl.BlockSpec((2, HG_WIDTH), lambda b, t: (0, 0)),
                pl.BlockSpec((1, HG_WIDTH), lambda b, t: (0, 0))]
    args = [hg3, hgrn_lb, g_hgrn.reshape(1, HG_WIDTH)]
    if has_s0:
        in_specs.append(st_spec)
        args.append(s0)
    return pl.pallas_call(
        functools.partial(_hgrn_kernel, tc=tc, has_s0=has_s0),
        grid=(nb, tl // tc),
        in_specs=in_specs,
        out_specs=[pl.BlockSpec((None, tc, HG_WIDTH), lambda b, t: (b, t, 0)), st_spec],
        out_shape=[jax.ShapeDtypeStruct((nb, tl, HG_WIDTH), BF16),
                   jax.ShapeDtypeStruct((nb, HG_HEADS, HG_DK, HG_DV), F32)],
        scratch_shapes=[pltpu.VMEM((HG_HEADS, HG_DV, HG_DK), F32)],
        compiler_params=_cparams(2),
        name="hgrn",
    )(*args)


def _merge_kernel(a_ref, b_ref, z_ref, x_ref, mod_ref, wa_ref, wb_ref, wo_ref, gpost_ref, gpre_ref,
                  wrh_ref, wrl_ref, br_ref, x1_ref, h2_ref, gate_ref, loc_ref, cnt_ref, *, tm, nseg, sub):
    i = pl.program_id(0)
    seg = tm // nseg

    def mod_rows(r, part):
        if nseg == 1:
            return mod_ref[0, r:r + 1, :]
        segs = range(part * sub // seg, (part + 1) * sub // seg)
        return jnp.concatenate([jnp.broadcast_to(mod_ref[s, r:r + 1, :], (seg, D_MODEL)) for s in segs], axis=0)

    @pl.when(i == 0)
    def _():
        cnt_ref[...] = jnp.zeros_like(cnt_ref)

    earlier = (lax.broadcasted_iota(I32, (sub, sub), 0) < lax.broadcasted_iota(I32, (sub, sub), 1)).astype(BF16)
    lower = (lax.broadcasted_iota(I32, (N_EXPERTS, N_EXPERTS), 1)
             < lax.broadcasted_iota(I32, (N_EXPERTS, N_EXPERTS), 0)).astype(BF16)
    eio = lax.broadcasted_iota(I32, (N_EXPERTS, sub), 0).astype(F32)
    tile_lane = lax.broadcasted_iota(I32, cnt_ref.shape, 1)
    for part in range(tm // sub):
        rs = slice(part * sub, (part + 1) * sub)
        za = z_ref[rs, 0:D_MODEL]
        zh = z_ref[rs, D_MODEL:2 * D_MODEL]
        m = (_sigmoid(za) * jnp.dot(a_ref[rs, :], wa_ref[...], preferred_element_type=F32)
             + _sigmoid(zh) * jnp.dot(b_ref[rs, :], wb_ref[...], preferred_element_type=F32))
        y = jnp.dot(m.astype(BF16), wo_ref[...], preferred_element_type=F32)
        x1 = x_ref[rs, :] + mod_rows(2, part) * (_rms(y) * gpost_ref[...])
        x1_ref[rs, :] = x1
        h2 = (_rms(x1) * gpre_ref[...]) * (1.0 + mod_rows(4, part)) + mod_rows(3, part)
        h_hi = h2.astype(BF16)
        h2_ref[rs, :] = h_hi

        h_lo = (h2 - h_hi.astype(F32)).astype(BF16)
        lg = (jnp.dot(h_hi, wrh_ref[...], preferred_element_type=F32)
              + jnp.dot(h_hi, wrl_ref[...], preferred_element_type=F32)
              + jnp.dot(h_lo, wrh_ref[...], preferred_element_type=F32))
        logits = lg.T[0:N_EXPERTS, :] + br_ref[...]
        vals, sels = [], []
        cur = logits
        for k in range(TOP_K):
            mx = jnp.max(cur, axis=0, keepdims=True)
            ik = jnp.min(jnp.where(cur == mx, eio, float(N_EXPERTS)), axis=0, keepdims=True)
            sel = eio == ik
            vals.append(mx)
            sels.append(sel)
            cur = jnp.where(sel, -jnp.inf, cur)
        es = [jnp.exp(v - vals[0]) for v in vals]
        den = es[0] + es[1] + es[2] + es[3]
        for k in range(TOP_K):
            gate_ref[k:k + 1, rs] = es[k] / den

        onehot = jnp.zeros(logits.shape, F32)
        for k in range(TOP_K):
            onehot = onehot + sels[k].astype(F32)
        n8 = jnp.ceil(jnp.sum(onehot, axis=1, keepdims=True) * 0.125) * 8.0
        base = (jnp.dot(lower, jnp.broadcast_to(n8, onehot.shape).astype(BF16), preferred_element_type=F32)
                + jnp.dot(onehot.astype(BF16), earlier, preferred_element_type=F32))
        for k in range(TOP_K):
            loc_ref[k:k + 1, rs] = jnp.sum(jnp.where(sels[k], base, 0.0), axis=0, keepdims=True).astype(I32)
        cnt_ref[...] = jnp.where(tile_lane == i * (tm // sub) + part, n8, cnt_ref[...])


def _merge(a2, b2, z2, x2, ada3, b_off, seq, wa, wb, wo, g_post, g_pre, wr_hi, wr_lo, br, tm, sub):
    n = x2.shape[0]
    if tm >= seq:
        nseg = tm // seq
        assert b_off % nseg == 0
        mod_idx = lambda i: (i + b_off // nseg, 0, 0)
    else:
        nseg = 1
        tpb = seq // tm
        mod_idx = lambda i: (i // tpb + b_off, 0, 0)
    row = lambda i: (i, 0)
    const = lambda i: (0, 0)
    col = lambda i: (0, i)
    return pl.pallas_call(
        functools.partial(_merge_kernel, tm=tm, nseg=nseg, sub=sub),
        grid=(n // tm,),
        in_specs=[pl.BlockSpec((tm, ATT_WIDTH), row),
                  pl.BlockSpec((tm, HG_WIDTH), row),
                  pl.BlockSpec((tm, 2 * D_MODEL), row),
                  pl.BlockSpec((tm, D_MODEL), row),
                  pl.BlockSpec((nseg, 6, D_MODEL), mod_idx),
                  pl.BlockSpec((ATT_WIDTH, D_MODEL), const),
                  pl.BlockSpec((HG_WIDTH, D_MODEL), const),
                  pl.BlockSpec((D_MODEL, D_MODEL), const),
                  pl.BlockSpec((1, D_MODEL), const),
                  pl.BlockSpec((1, D_MODEL), const),
                  pl.BlockSpec((D_MODEL, LANES), const),
                  pl.BlockSpec((D_MODEL, LANES), const),
                  pl.BlockSpec((N_EXPERTS, 1), const)],
        out_specs=[pl.BlockSpec((tm, D_MODEL), row),
                   pl.BlockSpec((tm, D_MODEL), row),
                   pl.BlockSpec((TOP_K, tm), col),
                   pl.BlockSpec((TOP_K, tm), col),
                   pl.BlockSpec((N_EXPERTS, n // sub), const)],
        out_shape=[jax.ShapeDtypeStruct((n, D_MODEL), F32),
                   jax.ShapeDtypeStruct((n, D_MODEL), BF16),
                   jax.ShapeDtypeStruct((TOP_K, n), F32),
                   jax.ShapeDtypeStruct((TOP_K, n), I32),
                   jax.ShapeDtypeStruct((N_EXPERTS, n // sub), F32)],
        compiler_params=_cparams(1),
        name="merge",
    )(a2, b2, z2, x2, ada3, wa, wb, wo, g_post.reshape(1, D_MODEL), g_pre.reshape(1, D_MODEL),
      wr_hi, wr_lo, br.reshape(N_EXPERTS, 1))


def _pow2_pieces(n, max_piece, fn):
    sz = max_piece
    while sz >= SUBLANES:
        off = n & ~(2 * sz - 1)

        @pl.when((n & sz) != 0)
        def _(off=off, sz=sz):
            fn(off, sz)

        sz //= 2


def _start_groups(meta_ref, td, make_copy):
    def body(e, _):
        loc0 = meta_ref[0, e]
        n8 = meta_ref[0, N_EXPERTS + e]
        glob0 = meta_ref[0, 2 * N_EXPERTS + e]
        _pow2_pieces(n8, td, lambda off, sz: make_copy(pl.multiple_of(loc0 + off, SUBLANES),
                                                       pl.multiple_of(glob0 + off, SUBLANES), sz).start())
        return 0

    lax.fori_loop(0, N_EXPERTS, body, 0, unroll=2)


def _wait_groups(meta_ref, n_sorted, make_wait):
    max_piece = 1 << (n_sorted.bit_length() - 1)
    _pow2_pieces(meta_ref[0, 3 * N_EXPERTS], max_piece, lambda off, sz: make_wait(sz).wait())


def _dispatch_kernel(meta_ref, mprev_ref, tail_ref, loc_ref, h_ref, buf_ref, srt_ref, zero_ref, sem, zsem,
                     *, td, blk):
    i = pl.program_id(0)
    slot = i % 2
    n_sorted = srt_ref.shape[1]

    @pl.when(i == 0)
    def _():
        zero_ref[...] = jnp.zeros_like(zero_ref)

        def zero_copy(start, off, sz):
            return pltpu.make_async_copy(zero_ref.at[pl.ds(0, sz)],
                                         buf_ref.at[pl.ds(pl.multiple_of(start + off, SUBLANES), sz)], zsem)

        for act in (lambda c: c.start(), lambda c: c.wait()):
            def body(e, _, act=act):
                start = tail_ref[0, e]
                _pow2_pieces(tail_ref[1, e], blk // 2, lambda off, sz: act(zero_copy(start, off, sz)))
                return 0

            lax.fori_loop(0, N_EXPERTS, body, 0)

            def unused(j, _, act=act):
                act(zero_copy(tail_ref[0, N_EXPERTS], j * (blk // 2), blk // 2))
                return 0

            lax.fori_loop(0, tail_ref[1, N_EXPERTS] // (blk // 2), unused, 0)

    rio = lax.broadcasted_iota(I32, (n_sorted, td), 0)
    hit = rio == loc_ref[0:1, :]
    for k in range(1, TOP_K):
        hit = hit | (rio == loc_ref[k:k + 1, :])
    srt_ref[slot] = jnp.dot(hit.astype(BF16), h_ref[...], preferred_element_type=F32)

    def make_copy(s):
        return lambda loc0, glob0, sz: pltpu.make_async_copy(
            srt_ref.at[s, pl.ds(loc0, sz)], buf_ref.at[pl.ds(glob0, sz)], sem.at[s])

    _start_groups(meta_ref, td, make_copy(slot))

    @pl.when(i > 0)
    def _():
        _wait_groups(mprev_ref, n_sorted, lambda sz: make_copy(1 - slot)(0, 0, sz))

    @pl.when(i == pl.num_programs(0) - 1)
    def _():
        _wait_groups(meta_ref, n_sorted, lambda sz: make_copy(slot)(0, 0, sz))


def _dispatch(h2, loc, meta, tail, n_rows, td, blk):
    n = h2.shape[0]
    n_sorted = TOP_K * td + SUBLANES * N_EXPERTS
    return pl.pallas_call(
        functools.partial(_dispatch_kernel, td=td, blk=blk),
        grid=(n // td,),
        in_specs=[pl.BlockSpec((None, 1, META_LEN), lambda i: (i, 0, 0), memory_space=pltpu.SMEM),
                  pl.BlockSpec((None, 1, META_LEN), lambda i: (jnp.maximum(i - 1, 0), 0, 0),
                               memory_space=pltpu.SMEM),
                  pl.BlockSpec(memory_space=pltpu.SMEM),
                  pl.BlockSpec((TOP_K, td), lambda i: (0, i)),
                  pl.BlockSpec((td, D_MODEL), lambda i: (i, 0))],
        out_specs=pl.BlockSpec(memory_space=pl.ANY),
        out_shape=jax.ShapeDtypeStruct((n_rows, D_MODEL), F32),
        scratch_shapes=[pltpu.VMEM((2, n_sorted, D_MODEL), F32), pltpu.VMEM((blk // 2, D_MODEL), F32),
                        pltpu.SemaphoreType.DMA((2,)), pltpu.SemaphoreType.DMA],
        compiler_params=_cparams(1),
        name="dispatch",
    )(meta, meta, tail, loc, h2)


def _expert_kernel(be_ref, nu_ref, x_ref, wgu_ref, bgu_ref, wd_ref, bd_ref, y_ref, wgu_bf, wd_bf):
    i = pl.program_id(0)

    @pl.when(i < nu_ref[0])
    def _():
        @pl.when((i == 0) | (be_ref[i] != be_ref[jnp.maximum(i - 1, 0)]))
        def _():
            wgu_bf[...] = wgu_ref[...].astype(BF16)
            wd_bf[...] = wd_ref[...].astype(BF16)

        gu = jnp.dot(x_ref[...].astype(BF16), wgu_bf[...], preferred_element_type=F32) + bgu_ref[...]
        gate = jnp.minimum(gu[:, :D_FF], SWIGLU_LIMIT)
        up = jnp.clip(gu[:, D_FF:], -SWIGLU_LIMIT, SWIGLU_LIMIT)
        act = gate * _sigmoid(SWIGLU_ALPHA * gate) * (up + 1.0)
        y_ref[...] = jnp.dot(act.astype(BF16), wd_bf[...], preferred_element_type=F32) + bd_ref[...]

    @pl.when(i >= nu_ref[0])
    def _():
        y_ref[...] = jnp.zeros_like(y_ref)


def _experts(buf, blk_expert, n_used, wgu, bgu, wd, bd, blk):
    n_rows = buf.shape[0]
    used = lambda i, be, nu: (jnp.minimum(i, nu[0] - 1), 0)
    grid_spec = pltpu.PrefetchScalarGridSpec(
        num_scalar_prefetch=2,
        grid=(n_rows // blk,),
        in_specs=[pl.BlockSpec((blk, D_MODEL), used),
                  pl.BlockSpec((None, D_MODEL, 2 * D_FF), lambda i, be, nu: (be[i], 0, 0)),
                  pl.BlockSpec((None, 1, 2 * D_FF), lambda i, be, nu: (be[i], 0, 0)),
                  pl.BlockSpec((None, D_FF, D_MODEL), lambda i, be, nu: (be[i], 0, 0)),
                  pl.BlockSpec((None, 1, D_MODEL), lambda i, be, nu: (be[i], 0, 0))],
        out_specs=pl.BlockSpec((blk, D_MODEL), lambda i, be, nu: (i, 0)),
        scratch_shapes=[pltpu.VMEM((D_MODEL, 2 * D_FF), BF16), pltpu.VMEM((D_FF, D_MODEL), BF16)],
    )
    return pl.pallas_call(
        _expert_kernel,
        grid_spec=grid_spec,
        out_shape=jax.ShapeDtypeStruct((n_rows, D_MODEL), F32),
        compiler_params=_cparams(1),
        name="experts",
    )(blk_expert, n_used, buf, wgu, bgu.reshape(N_EXPERTS, 1, 2 * D_FF), wd, bd.reshape(N_EXPERTS, 1, D_MODEL))


def _combine_kernel(meta_ref, mnext_ref, loc_ref, gate_ref, x1_ref, mod_ref, gpost_ref, ybuf_ref, o_ref,
                    ys_ref, sem, *, td, nseg):
    i = pl.program_id(0)
    slot = i % 2
    n_sorted = ys_ref.shape[1]

    def make_copy(s):
        return lambda loc0, glob0, sz: pltpu.make_async_copy(
            ybuf_ref.at[pl.ds(glob0, sz)], ys_ref.at[s, pl.ds(loc0, sz)], sem.at[s])

    @pl.when(i == 0)
    def _():
        ys_ref[...] = jnp.zeros_like(ys_ref)
        _start_groups(meta_ref, td, make_copy(0))

    @pl.when(i + 1 < pl.num_programs(0))
    def _():
        _start_groups(mnext_ref, td, make_copy(1 - slot))

    _wait_groups(meta_ref, n_sorted, lambda sz: make_copy(slot)(0, 0, sz))
    y = ys_ref[slot].astype(BF16)
    jio = lax.broadcasted_iota(I32, (td, n_sorted), 1)
    w = jnp.zeros((td, n_sorted), F32)
    for k in range(TOP_K):
        w = w + jnp.where(jio == loc_ref[:, k:k + 1], gate_ref[:, k:k + 1], 0.0)
    w_hi = w.astype(BF16)
    w_lo = (w - w_hi.astype(F32)).astype(BF16)
    mo = jnp.dot(w_hi, y, preferred_element_type=F32) + jnp.dot(w_lo, y, preferred_element_type=F32)
    if nseg == 1:
        gate2 = mod_ref[0, 5:6, :]
    else:
        gate2 = jnp.concatenate([jnp.broadcast_to(mod_ref[s, 5:6, :], (td // nseg, D_MODEL)) for s in range(nseg)],
                                axis=0)
    o_ref[...] = x1_ref[...] + gate2 * (_rms(mo) * gpost_ref[...])


def _combine(meta, loc_t, gates_t, x1, ada3, b_off, seq, g_post, ybuf, td):
    n = x1.shape[0]
    n_tiles = n // td
    n_sorted = TOP_K * td + SUBLANES * N_EXPERTS
    if td >= seq:
        nseg = td // seq
        assert b_off % nseg == 0
        mod_idx = lambda i: (i + b_off // nseg, 0, 0)
    else:
        nseg = 1
        tpb = seq // td
        mod_idx = lambda i: (i // tpb + b_off, 0, 0)
    return pl.pallas_call(
        functools.partial(_combine_kernel, td=td, nseg=nseg),
        grid=(n_tiles,),
        in_specs=[pl.BlockSpec((None, 1, META_LEN), lambda i: (i, 0, 0), memory_space=pltpu.SMEM),
                  pl.BlockSpec((None, 1, META_LEN), lambda i: (jnp.minimum(i + 1, n_tiles - 1), 0, 0),
                               memory_space=pltpu.SMEM),
                  pl.BlockSpec((td, TOP_K), lambda i: (i, 0)),
                  pl.BlockSpec((td, TOP_K), lambda i: (i, 0)),
                  pl.BlockSpec((td, D_MODEL), lambda i: (i, 0)),
                  pl.BlockSpec((nseg, 6, D_MODEL), mod_idx),
                  pl.BlockSpec((1, D_MODEL), lambda i: (0, 0)),
                  pl.BlockSpec(memory_space=pl.ANY)],
        out_specs=pl.BlockSpec((td, D_MODEL), lambda i: (i, 0)),
        out_shape=jax.ShapeDtypeStruct((n, D_MODEL), F32),
        scratch_shapes=[pltpu.VMEM((2, n_sorted, D_MODEL), F32), pltpu.SemaphoreType.DMA((2,))],
        compiler_params=_cparams(1),
        name="combine",
    )(meta, meta, loc_t, gates_t, x1, ada3, g_post.reshape(1, D_MODEL), ybuf)


def _tile_sizes(nb, seq):
    n = nb * seq
    big = n >= 8192
    tmm = 512 if seq >= 512 else n
    return dict(
        tm=min(seq, 256),
        tmm=tmm,
        tq=min(seq, 256),
        tc=min(seq, 512),
        td=min(tmm, 256),
        blk=512 if big else 128,
    )


def _rope_tables(pos):
    half = HEAD_DIM // 2
    inv_freq = ROPE_THETA ** (-jnp.arange(half, dtype=F32) / half)
    ang = pos.astype(F32)[:, None] * inv_freq[None, :]
    cos, sin = jnp.cos(ang), jnp.sin(ang)
    reps = LANES // HEAD_DIM
    return (jnp.tile(jnp.concatenate([cos, cos], axis=1), (1, reps)),
            jnp.tile(jnp.concatenate([-sin, sin], axis=1), (1, reps)))


def _route_plan(cnt8, n_assign, blk):
    n_tiles = cnt8.shape[1]
    n_blocks = -(-(n_assign + n_tiles * N_EXPERTS * (SUBLANES - 1) + N_EXPERTS * (blk - 1)) // blk)
    tot = jnp.sum(cnt8, axis=1)
    padded = (tot + blk - 1) // blk * blk
    pend = jnp.cumsum(padded)
    pstart = pend - padded
    glob0 = pstart[:, None] + jnp.cumsum(cnt8, axis=1) - cnt8
    loc0 = jnp.cumsum(cnt8, axis=0) - cnt8
    meta = jnp.concatenate([loc0.T, cnt8.T, glob0.T, jnp.sum(cnt8, axis=0)[:, None]],
                           axis=1).reshape(n_tiles, 1, META_LEN)
    blk_row = jnp.arange(n_blocks, dtype=I32) * blk
    blk_expert = jnp.minimum(jnp.sum((pend[None, :] <= blk_row[:, None]).astype(I32), axis=1), N_EXPERTS - 1)
    n_used = (pend[-1:] // blk).astype(I32)
    n_rows = n_blocks * blk
    tail = jnp.stack([jnp.concatenate([pstart + tot, pend[-1:]]),
                      jnp.concatenate([padded - tot, n_rows - pend[-1:]])])
    return meta.astype(I32), tail.astype(I32), blk_expert, n_used, n_rows


def _layer(x, ada3, b_off, pos, k_past, v_past, s0, wts):
    nb, seq, _ = x.shape
    n = nb * seq
    ts = _tile_sizes(nb, seq)
    x2 = x.reshape(n, D_MODEL)
    cos_t, sin_t = _rope_tables(pos)
    q, k, v, hg, z = _proj(x2, ada3, b_off, seq, wts['g_pre_mix'], wts['w_in'], cos_t, sin_t, ts['tm'])
    k3 = k.reshape(nb, seq, KV_WIDTH)
    v3 = v.reshape(nb, seq, KV_WIDTH)
    if k_past is None:
        koff = 0
    else:
        rows = k_past.shape[1]
        koff = rows
        k3 = jnp.concatenate([k_past.reshape(nb, rows, KV_WIDTH), k3], axis=1)
        v3 = jnp.concatenate([v_past.reshape(nb, rows, KV_WIDTH), v3], axis=1)
    y_att = _attn(q.reshape(nb, seq, ATT_WIDTH), k3, v3, wts['sinks'], ts['tq'], koff)
    y_hg, s_new = _hgrn(hg.reshape(nb, seq, 4 * HG_WIDTH), wts['hgrn_lb'], wts['g_hgrn'], s0, ts['tc'])
    td, blk = ts['td'], ts['blk']
    x1, h2, gates, loc, cnt = _merge(
        y_att.reshape(n, ATT_WIDTH), y_hg.reshape(n, HG_WIDTH), z, x2, ada3, b_off, seq,
        wts['w_br_attn'], wts['w_br_hgrn'], wts['w_out'], wts['g_post_mix'], wts['g_pre_ffn'],
        wts['w_router_hi'], wts['w_router_lo'], wts['b_router'], ts['tmm'], td)
    meta, tail, blk_expert, n_used, n_rows = _route_plan(cnt.astype(I32), n * TOP_K, blk)
    buf = _dispatch(h2, loc, meta, tail, n_rows, td, blk)
    ybuf = _experts(buf, blk_expert, n_used, wts['w_gate_up'], wts['b_gate_up'], wts['w_down'], wts['b_down'], blk)
    out = _combine(meta, loc.T, gates.T, x1, ada3, b_off, seq, wts['g_post_ffn'], ybuf, td)
    win = min(WINDOW, k3.shape[1])
    k_new = k3[:, k3.shape[1] - win:].reshape(nb, win, N_KV_HEADS, HEAD_DIM)
    v_new = v3[:, v3.shape[1] - win:].reshape(nb, win, N_KV_HEADS, HEAD_DIM)
    return out.reshape(nb, seq, D_MODEL), k_new, v_new, s_new


def kernel(x_prompt, x_sample, cache_k, cache_v, state_hgrn, c_prompt, c_sample, w_ada, b_ada, g_pre_mix, g_post_mix, g_pre_ffn, g_post_ffn, w_in, attn_sinks, hgrn_lb, g_hgrn, w_br_attn, w_br_hgrn, w_out, w_router, b_router, w_gate_up, b_gate_up, w_down, b_down):
    n_bp = x_prompt.shape[0]
    wr = jnp.pad(w_router[0], ((0, 0), (0, LANES - N_EXPERTS)))
    wr_hi = wr.astype(BF16)
    wr_lo = (wr - wr_hi.astype(F32)).astype(BF16)
    wts = dict(
        g_pre_mix=g_pre_mix[0], g_post_mix=g_post_mix[0], g_pre_ffn=g_pre_ffn[0], g_post_ffn=g_post_ffn[0],
        w_in=w_in[0].astype(BF16), sinks=attn_sinks[0], hgrn_lb=hgrn_lb, g_hgrn=g_hgrn[0],
        w_br_attn=w_br_attn[0].astype(BF16), w_br_hgrn=w_br_hgrn[0].astype(BF16), w_out=w_out[0].astype(BF16),
        w_router_hi=wr_hi, w_router_lo=wr_lo, b_router=b_router[0],
        w_gate_up=w_gate_up[0], b_gate_up=b_gate_up[0], w_down=w_down[0], b_down=b_down[0])
    ada = _ada(jnp.concatenate([c_prompt, c_sample], axis=0), w_ada[0], b_ada[0])
    ada3 = ada.reshape(ada.shape[0], 6, D_MODEL)
    pos_p = jnp.arange(x_prompt.shape[1])
    pos_s = PAST_LEN + jnp.arange(x_sample.shape[1])
    ys, ks, vs, ss = _layer(x_sample, ada3, n_bp, pos_s, cache_k[0], cache_v[0], state_hgrn[0], wts)
    yp, kp, vp, sp = _layer(x_prompt, ada3, 0, pos_p, None, None, None, wts)
    return (yp, ys, kp[None], vp[None], sp[None], ks[None], vs[None], ss[None])
```

```python
import functools

import jax
import jax.numpy as jnp
from jax import lax
from jax.experimental import pallas as pl
from jax.experimental.pallas import tpu as pltpu

F32 = jnp.float32
BF16 = jnp.bfloat16
I32 = jnp.int32

D_MODEL = 1024
PAST_LEN = 1024
CHUNK = 64
HEAD_DIM = 64
N_Q_HEADS = 8
N_KV_HEADS = 2
Q_PER_KV = N_Q_HEADS // N_KV_HEADS
ATT_WIDTH = N_Q_HEADS * HEAD_DIM
KV_WIDTH = N_KV_HEADS * HEAD_DIM
WINDOW = 128
ROPE_THETA = 10000.0
HG_HEADS = 4
HG_DK = 128
HG_DV = 128
HG_WIDTH = HG_HEADS * HG_DK
HG_BLOCK = 16
N_EXPERTS = 32
TOP_K = 4
D_FF = 1024
SWIGLU_LIMIT = 7.0
SWIGLU_ALPHA = 1.702
RMS_EPS = 1e-6
N_IN = ATT_WIDTH + 2 * KV_WIDTH + 4 * HG_WIDTH + 2 * D_MODEL
HG_OFF = ATT_WIDTH + 2 * KV_WIDTH
Z_OFF = HG_OFF + 4 * HG_WIDTH

ATTN_PHASE_CHUNKS = 8
META_LEN = 3 * N_EXPERTS + 1
LANES = 128
SUBLANES = 8
VMEM_LIMIT = 56 * 1024 * 1024

_NT = (((1,), (1,)), ((), ()))
_TN = (((0,), (0,)), ((), ()))


def _sigmoid(x):
    return 1.0 / (1.0 + jnp.exp(-x))


def _rms(x):
    return x * lax.rsqrt(jnp.mean(x * x, axis=-1, keepdims=True) + RMS_EPS)


def _cparams(n_axes):
    return pltpu.CompilerParams(dimension_semantics=("arbitrary",) * n_axes,
                                vmem_limit_bytes=VMEM_LIMIT)


def _ada_kernel(c_ref, w_ref, b_ref, o_ref):
    c = c_ref[...]
    s = c * _sigmoid(c)
    o_ref[...] = jnp.dot(s, w_ref[...], precision=lax.Precision.HIGHEST,
                         preferred_element_type=F32) + b_ref[...]


def _ada(c, w_ada, b_ada):
    nb = c.shape[0]
    n_out = w_ada.shape[1]
    tn = D_MODEL
    return pl.pallas_call(
        _ada_kernel,
        grid=(n_out // tn,),
        in_specs=[pl.BlockSpec((nb, D_MODEL), lambda j: (0, 0)),
                  pl.BlockSpec((D_MODEL, tn), lambda j: (0, j)),
                  pl.BlockSpec((1, tn), lambda j: (0, j))],
        out_specs=pl.BlockSpec((nb, tn), lambda j: (0, j)),
        out_shape=jax.ShapeDtypeStruct((nb, n_out), F32),
        compiler_params=_cparams(1),
        name="ada",
    )(c, w_ada, b_ada.reshape(1, n_out))


def _proj_kernel(x_ref, mod_ref, g_ref, w_ref, cos_ref, sin_ref,
                 q_ref, k_ref, v_ref, hg_ref, z_ref):
    x = x_ref[...]
    h = _rms(x) * g_ref[...]
    h = h * (1.0 + mod_ref[1:2, :]) + mod_ref[0:1, :]
    hb = h.astype(BF16)
    cos = cos_ref[...]
    sin = sin_ref[...]
    lane = lax.broadcasted_iota(I32, cos.shape, 1)
    first_half = (lane % HEAD_DIM) < (HEAD_DIM // 2)

    def cols(start, width):
        return jnp.dot(hb, w_ref[:, start:start + width], preferred_element_type=F32)

    def rope(p):
        partner = jnp.where(first_half, pltpu.roll(p, LANES - HEAD_DIM // 2, 1),
                            pltpu.roll(p, HEAD_DIM // 2, 1))
        return p * cos + partner * sin

    for j in range(ATT_WIDTH // LANES):
        q = rope(cols(j * LANES, LANES)) * (HEAD_DIM ** -0.5)
        q_ref[:, j * LANES:(j + 1) * LANES] = q.astype(BF16)
    k_ref[...] = rope(cols(ATT_WIDTH, KV_WIDTH))
    v_ref[...] = cols(ATT_WIDTH + KV_WIDTH, KV_WIDTH)
    for j in range(4):
        hg_ref[:, j * HG_WIDTH:(j + 1) * HG_WIDTH] = cols(HG_OFF + j * HG_WIDTH, HG_WIDTH)
    for j in range(4):
        z_ref[:, j * 512:(j + 1) * 512] = cols(Z_OFF + j * 512, 512)


def _proj(x2, ada3, b_off, seq, g_pre, w_in_bf, cos_t, sin_t, tm):
    n = x2.shape[0]
    tpb = seq // tm
    row = lambda i: (i, 0)
    return pl.pallas_call(
        _proj_kernel,
        grid=(n // tm,),
        in_specs=[pl.BlockSpec((tm, D_MODEL), row),
                  pl.BlockSpec((None, 6, D_MODEL), lambda i: (i // tpb + b_off, 0, 0)),
                  pl.BlockSpec((1, D_MODEL), lambda i: (0, 0)),
                  pl.BlockSpec((D_MODEL, N_IN), lambda i: (0, 0)),
                  pl.BlockSpec((tm, LANES), lambda i: (i % tpb, 0)),
                  pl.BlockSpec((tm, LANES), lambda i: (i % tpb, 0))],
        out_specs=[pl.BlockSpec((tm, ATT_WIDTH), row),
                   pl.BlockSpec((tm, KV_WIDTH), row),
                   pl.BlockSpec((tm, KV_WIDTH), row),
                   pl.BlockSpec((tm, 4 * HG_WIDTH), row),
                   pl.BlockSpec((tm, 2 * D_MODEL), row)],
        out_shape=[jax.ShapeDtypeStruct((n, ATT_WIDTH), BF16),
                   jax.ShapeDtypeStruct((n, KV_WIDTH), F32),
                   jax.ShapeDtypeStruct((n, KV_WIDTH), F32),
                   jax.ShapeDtypeStruct((n, 4 * HG_WIDTH), F32),
                   jax.ShapeDtypeStruct((n, 2 * D_MODEL), F32)],
        compiler_params=_cparams(1),
        name="proj",
    )(x2, ada3, g_pre.reshape(1, D_MODEL), w_in_bf, cos_t, sin_t)


def _attn_kernel(sink_ref, q_ref, k_ref, v_ref, o_ref, *, tq, koff):
    t = pl.program_id(1)
    span = WINDOW + CHUNK
    lane = lax.broadcasted_iota(I32, (span, KV_WIDTH), 1)
    top_row = lax.broadcasted_iota(I32, (2 * CHUNK, 1), 0) < CHUNK
    n_chunks = tq // CHUNK
    for c0 in range(0, n_chunks, ATTN_PHASE_CHUNKS):
        units = []
        for c in range(c0, min(c0 + ATTN_PHASE_CHUNKS, n_chunks)):
            r = t * tq + c * CHUNK + koff
            ks = pl.multiple_of(jnp.maximum(r - WINDOW, 0), CHUNK)
            kf = k_ref[pl.ds(ks, span), :]
            vf = v_ref[pl.ds(ks, span), :]
            valid = ks + lax.broadcasted_iota(I32, (1, span), 1) < r + CHUNK
            rows = slice(c * CHUNK, (c + 1) * CHUNK)
            for g in range(N_KV_HEADS):
                own = (lane // HEAD_DIM) == g
                k_own = jnp.where(own, kf, 0.0)
                v_own = jnp.where(own, vf, 0.0)
                k_oth = pltpu.roll(k_own, HEAD_DIM, 1)
                v_oth = pltpu.roll(v_own, HEAD_DIM, 1)
                k_par = (k_own, k_oth) if g == 0 else (k_oth, k_own)
                v_par = (v_own, v_oth) if g == 0 else (v_oth, v_own)
                qq = jnp.concatenate([q_ref[rows, (2 * g + i) * LANES:(2 * g + i + 1) * LANES] for i in range(2)],
                                     axis=0)
                for par in range(2):
                    h_top = Q_PER_KV * g + par
                    units.append((qq, k_par[par].astype(BF16), v_par[par].astype(BF16),
                                  jnp.where(top_row, sink_ref[h_top], sink_ref[h_top + 2]), valid))
        ss = [jnp.where(u[4], lax.dot_general(u[0], u[1], _NT, preferred_element_type=F32), -jnp.inf) for u in units]
        ms = [jnp.maximum(jnp.max(s, axis=-1, keepdims=True), u[3]) for s, u in zip(ss, units)]
        ps = [jnp.exp(s - m) for s, m in zip(ss, ms)]
        ds = [jnp.sum(p, axis=-1, keepdims=True) + jnp.exp(u[3] - m) for p, m, u in zip(ps, ms, units)]
        os_ = [jnp.dot((p / d).astype(BF16), u[2], preferred_element_type=F32) for p, d, u in zip(ps, ds, units)]
        for ci, c in enumerate(range(c0, min(c0 + ATTN_PHASE_CHUNKS, n_chunks))):
            rows = slice(c * CHUNK, (c + 1) * CHUNK)
            for g in range(N_KV_HEADS):
                u0 = (ci * N_KV_HEADS + g) * 2
                acc = os_[u0] + os_[u0 + 1]
                for i in range(2):
                    o_ref[rows, (2 * g + i) * LANES:(2 * g + i + 1) * LANES] = (
                        acc[i * CHUNK:(i + 1) * CHUNK].astype(BF16))


def _attn(q3, k3, v3, sinks, tq, koff):
    nb, tl, _ = q3.shape
    tk = k3.shape[1]
    return pl.pallas_call(
        functools.partial(_attn_kernel, tq=tq, koff=koff),
        grid=(nb, tl // tq),
        in_specs=[pl.BlockSpec(memory_space=pltpu.SMEM),
                  pl.BlockSpec((None, tq, ATT_WIDTH), lambda b, t: (b, t, 0)),
                  pl.BlockSpec((None, tk, KV_WIDTH), lambda b, t: (b, 0, 0)),
                  pl.BlockSpec((None, tk, KV_WIDTH), lambda b, t: (b, 0, 0))],
        out_specs=pl.BlockSpec((None, tq, ATT_WIDTH), lambda b, t: (b, t, 0)),
        out_shape=jax.ShapeDtypeStruct((nb, tl, ATT_WIDTH), BF16),
        compiler_params=_cparams(2),
        name="attn",
    )(sinks, q3, k3, v3)


def _hgrn_kernel(*refs, tc, has_s0):
    if has_s0:
        hg_ref, lbp_ref, gh_ref, s0_ref, y_ref, s_out_ref, st_ref = refs
    else:
        hg_ref, lbp_ref, gh_ref, y_ref, s_out_ref, st_ref = refs
        s0_ref = None
    t = pl.program_id(1)

    @pl.when(t == 0)
    def _():
        for h in range(HG_HEADS):
            st_ref[h] = s0_ref[h].T if has_s0 else jnp.zeros((HG_DV, HG_DK), F32)

    l0 = lbp_ref[0:1, :]
    l1 = lbp_ref[1:2, :]
    lm = jnp.maximum(l0, l1)
    e0 = jnp.exp(l0 - lm)
    lb = e0 / (e0 + jnp.exp(l1 - lm))

    hq = hg_ref[:, 0:HG_WIDTH]
    hf = hg_ref[:, HG_WIDTH:2 * HG_WIDTH]
    hi = hg_ref[:, 2 * HG_WIDTH:3 * HG_WIDTH]
    hz = hg_ref[:, 3 * HG_WIDTH:4 * HG_WIDTH]
    f = lb + (1.0 - lb) * _sigmoid(hf)
    q = hq * _sigmoid(hq)
    kk = 1.0 - f
    g = jnp.log(f)

    r16 = lax.broadcasted_iota(I32, g.shape, 0) % HG_BLOCK
    b = g
    suf = g
    for s in (1, 2, 4, 8):
        b = b + jnp.where(r16 >= s, pltpu.roll(b, s, 0), 0.0)
        suf = suf + jnp.where(r16 < HG_BLOCK - s, pltpu.roll(suf, tc - s, 0), 0.0)
    qt = (q * jnp.exp(b)).astype(BF16)
    kt = (kk * jnp.exp(-b)).astype(BF16)
    kd = (kk * jnp.exp(suf - g)).astype(BF16)
    vb = hi.astype(BF16)
    grp = min(tc, LANES)
    gi = lax.broadcasted_iota(I32, (grp, grp), 0)
    gj = lax.broadcasted_iota(I32, (grp, grp), 1)
    keep = ((gi // HG_BLOCK) == (gj // HG_BLOCK)) & (gi >= gj)
    heads = [slice(h * HG_DK, (h + 1) * HG_DK) for h in range(HG_HEADS)]
    o_groups = [[] for _ in heads]
    for g in range(tc // grp):
        gs = slice(g * grp, (g + 1) * grp)
        o_intra = []
        for cs in heads:
            a = lax.dot_general(qt[gs, cs], kt[gs, cs], _NT, preferred_element_type=F32)
            a = jnp.where(keep, a, 0.0)
            o_intra.append(jnp.dot(a.astype(BF16), vb[gs, cs], preferred_element_type=F32))
        o_inter = [[] for _ in heads]
        for j in range(grp // HG_BLOCK):
            r0 = g * grp + j * HG_BLOCK
            rs = slice(r0, r0 + HG_BLOCK)
            for h, cs in enumerate(heads):
                st = st_ref[h]
                o_inter[h].append(lax.dot_general(qt[rs, cs], st.astype(BF16), _NT, preferred_element_type=F32))
                dec = jnp.exp(b[r0 + HG_BLOCK - 1:r0 + HG_BLOCK, cs])
                st_ref[h] = st * dec + lax.dot_general(vb[rs, cs], kd[rs, cs], _TN, preferred_element_type=F32)
        for h in range(HG_HEADS):
            o_groups[h].append(o_intra[h] + jnp.concatenate(o_inter[h], axis=0))
    o_heads = []
    for h, cs in enumerate(heads):
        o_h = jnp.concatenate(o_groups[h], axis=0)
        zs = hz[:, cs]
        o_heads.append(_rms(o_h) * gh_ref[:, cs] * (zs * _sigmoid(zs)))
    y_ref[...] = jnp.concatenate(o_heads, axis=1).astype(BF16)

    @pl.when(t == pl.num_programs(1) - 1)
    def _():
        for h in range(HG_HEADS):
            s_out_ref[h] = st_ref[h].T


def _hgrn(hg3, hgrn_lb, g_hgrn, s0, tc):
    nb, tl, _ = hg3.shape
    has_s0 = s0 is not None
    st_spec = pl.BlockSpec((None, HG_HEADS, HG_DK, HG_DV), lambda b, t: (b, 0, 0, 0))
    in_specs = [pl.BlockSpec((None, tc, 4 * HG_WIDTH), lambda b, t: (b, t, 0)),
                pl.BlockSpec((2, HG_WIDTH), lambda b, t: (0, 0)),
                pl.BlockSpec((1, HG_WIDTH), lambda b, t: (0, 0))]
    args = [hg3, hgrn_lb, g_hgrn.reshape(1, HG_WIDTH)]
    if has_s0:
        in_specs.append(st_spec)
        args.append(s0)
    return pl.pallas_call(
        functools.partial(_hgrn_kernel, tc=tc, has_s0=has_s0),
        grid=(nb, tl // tc),
        in_specs=in_specs,
        out_specs=[pl.BlockSpec((None, tc, HG_WIDTH), lambda b, t: (b, t, 0)), st_spec],
        out_shape=[jax.ShapeDtypeStruct((nb, tl, HG_WIDTH), BF16),
                   jax.ShapeDtypeStruct((nb, HG_HEADS, HG_DK, HG_DV), F32)],
        scratch_shapes=[pltpu.VMEM((HG_HEADS, HG_DV, HG_DK), F32)],
        compiler_params=_cparams(2),
        name="hgrn",
    )(*args)


def _merge_kernel(a_ref, b_ref, z_ref, x_ref, mod_ref, wa_ref, wb_ref, wo_ref, gpost_ref, gpre_ref,
                  wrh_ref, wrl_ref, br_ref, x1_ref, h2_ref, gate_ref, loc_ref, cnt_ref, *, tm, nseg, sub):
    i = pl.program_id(0)
    seg = tm // nseg

    def mod_rows(r, part):
        if nseg == 1:
            return mod_ref[0, r:r + 1, :]
        segs = range(part * sub // seg, (part + 1) * sub // seg)
        return jnp.concatenate([jnp.broadcast_to(mod_ref[s, r:r + 1, :], (seg, D_MODEL)) for s in segs], axis=0)

    @pl.when(i == 0)
    def _():
        cnt_ref[...] = jnp.zeros_like(cnt_ref)

    earlier = (lax.broadcasted_iota(I32, (sub, sub), 0) < lax.broadcasted_iota(I32, (sub, sub), 1)).astype(BF16)
    lower = (lax.broadcasted_iota(I32, (N_EXPERTS, N_EXPERTS), 1)
             < lax.broadcasted_iota(I32, (N_EXPERTS, N_EXPERTS), 0)).astype(BF16)
    eio = lax.broadcasted_iota(I32, (N_EXPERTS, sub), 0).astype(F32)
    tile_lane = lax.broadcasted_iota(I32, cnt_ref.shape, 1)
    parts = range(tm // sub)
    rss = [slice(part * sub, (part + 1) * sub) for part in parts]
    ms = [(_sigmoid(z_ref[rs, 0:D_MODEL]) * jnp.dot(a_ref[rs, :], wa_ref[...], preferred_element_type=F32)
           + _sigmoid(z_ref[rs, D_MODEL:2 * D_MODEL]) * jnp.dot(b_ref[rs, :], wb_ref[...], preferred_element_type=F32))
          for rs in rss]
    ys = [jnp.dot(m.astype(BF16), wo_ref[...], preferred_element_type=F32) for m in ms]
    x1s = [x_ref[rs, :] + mod_rows(2, part) * (_rms(y) * gpost_ref[...]) for part, rs, y in zip(parts, rss, ys)]
    for rs, x1 in zip(rss, x1s):
        x1_ref[rs, :] = x1
    h2s = [(_rms(x1) * gpre_ref[...]) * (1.0 + mod_rows(4, part)) + mod_rows(3, part) for part, x1 in zip(parts, x1s)]
    h_his = [h2.astype(BF16) for h2 in h2s]
    for rs, h_hi in zip(rss, h_his):
        h2_ref[rs, :] = h_hi

    h_los = [(h2 - h_hi.astype(F32)).astype(BF16) for h2, h_hi in zip(h2s, h_his)]
    lgs = [(jnp.dot(h_hi, wrh_ref[...], preferred_element_type=F32)
            + jnp.dot(h_hi, wrl_ref[...], preferred_element_type=F32)
            + jnp.dot(h_lo, wrh_ref[...], preferred_element_type=F32)) for h_hi, h_lo in zip(h_his, h_los)]
    curs = [lg.T[0:N_EXPERTS, :] + br_ref[...] for lg in lgs]
    vals = [[] for _ in parts]
    sels = [[] for _ in parts]
    for k in range(TOP_K):
        for part in parts:
            cur = curs[part]
            mx = jnp.max(cur, axis=0, keepdims=True)
            ik = jnp.min(jnp.where(cur == mx, eio, float(N_EXPERTS)), axis=0, keepdims=True)
            sel = eio == ik
            vals[part].append(mx)
            sels[part].append(sel)
            curs[part] = jnp.where(sel, -jnp.inf, cur)
    for part, rs in zip(parts, rss):
        es = [jnp.exp(v - vals[part][0]) for v in vals[part]]
        den = es[0] + es[1] + es[2] + es[3]
        for k in range(TOP_K):
            gate_ref[k:k + 1, rs] = es[k] / den

    for part, rs in zip(parts, rss):
        onehot = jnp.zeros((N_EXPERTS, sub), F32)
        for k in range(TOP_K):
            onehot = onehot + sels[part][k].astype(F32)
        n8 = jnp.ceil(jnp.sum(onehot, axis=1, keepdims=True) * 0.125) * 8.0
        base = (jnp.dot(lower, jnp.broadcast_to(n8, onehot.shape).astype(BF16), preferred_element_type=F32)
                + jnp.dot(onehot.astype(BF16), earlier, preferred_element_type=F32))
        for k in range(TOP_K):
            loc_ref[k:k + 1, rs] = jnp.sum(jnp.where(sels[part][k], base, 0.0), axis=0, keepdims=True).astype(I32)
        cnt_ref[...] = jnp.where(tile_lane == i * (tm // sub) + part, n8, cnt_ref[...])


def _merge(a2, b2, z2, x2, ada3, b_off, seq, wa, wb, wo, g_post, g_pre, wr_hi, wr_lo, br, tm, sub):
    n = x2.shape[0]
    if tm >= seq:
        nseg = tm // seq
        assert b_off % nseg == 0
        mod_idx = lambda i: (i + b_off // nseg, 0, 0)
    else:
        nseg = 1
        tpb = seq // tm
        mod_idx = lambda i: (i // tpb + b_off, 0, 0)
    row = lambda i: (i, 0)
    const = lambda i: (0, 0)
    col = lambda i: (0, i)
    return pl.pallas_call(
        functools.partial(_merge_kernel, tm=tm, nseg=nseg, sub=sub),
        grid=(n // tm,),
        in_specs=[pl.BlockSpec((tm, ATT_WIDTH), row),
                  pl.BlockSpec((tm, HG_WIDTH), row),
                  pl.BlockSpec((tm, 2 * D_MODEL), row),
                  pl.BlockSpec((tm, D_MODEL), row),
                  pl.BlockSpec((nseg, 6, D_MODEL), mod_idx),
                  pl.BlockSpec((ATT_WIDTH, D_MODEL), const),
                  pl.BlockSpec((HG_WIDTH, D_MODEL), const),
                  pl.BlockSpec((D_MODEL, D_MODEL), const),
                  pl.BlockSpec((1, D_MODEL), const),
                  pl.BlockSpec((1, D_MODEL), const),
                  pl.BlockSpec((D_MODEL, LANES), const),
                  pl.BlockSpec((D_MODEL, LANES), const),
                  pl.BlockSpec((N_EXPERTS, 1), const)],
        out_specs=[pl.BlockSpec((tm, D_MODEL), row),
                   pl.BlockSpec((tm, D_MODEL), row),
                   pl.BlockSpec((TOP_K, tm), col),
                   pl.BlockSpec((TOP_K, tm), col),
                   pl.BlockSpec((N_EXPERTS, n // sub), const)],
        out_shape=[jax.ShapeDtypeStruct((n, D_MODEL), F32),
                   jax.ShapeDtypeStruct((n, D_MODEL), BF16),
                   jax.ShapeDtypeStruct((TOP_K, n), F32),
                   jax.ShapeDtypeStruct((TOP_K, n), I32),
                   jax.ShapeDtypeStruct((N_EXPERTS, n // sub), F32)],
        compiler_params=_cparams(1),
        name="merge",
    )(a2, b2, z2, x2, ada3, wa, wb, wo, g_post.reshape(1, D_MODEL), g_pre.reshape(1, D_MODEL),
      wr_hi, wr_lo, br.reshape(N_EXPERTS, 1))


def _pow2_pieces(n, max_piece, fn):
    sz = max_piece
    while sz >= SUBLANES:
        off = n & ~(2 * sz - 1)

        @pl.when((n & sz) != 0)
        def _(off=off, sz=sz):
            fn(off, sz)

        sz //= 2


def _start_groups(meta_ref, td, make_copy):
    def body(e, _):
        loc0 = meta_ref[0, e]
        n8 = meta_ref[0, N_EXPERTS + e]
        glob0 = meta_ref[0, 2 * N_EXPERTS + e]
        _pow2_pieces(n8, td, lambda off, sz: make_copy(pl.multiple_of(loc0 + off, SUBLANES),
                                                       pl.multiple_of(glob0 + off, SUBLANES), sz).start())
        return 0

    lax.fori_loop(0, N_EXPERTS, body, 0, unroll=2)


def _wait_groups(meta_ref, n_sorted, make_wait):
    max_piece = 1 << (n_sorted.bit_length() - 1)
    _pow2_pieces(meta_ref[0, 3 * N_EXPERTS], max_piece, lambda off, sz: make_wait(sz).wait())


def _dispatch_kernel(meta_ref, mprev_ref, tail_ref, loc_ref, h_ref, buf_ref, srt_ref, zero_ref, sem, zsem,
                     *, td, blk):
    i = pl.program_id(0)
    slot = i % 2
    n_sorted = srt_ref.shape[1]

    @pl.when(i == 0)
    def _():
        zero_ref[...] = jnp.zeros_like(zero_ref)

        def zero_copy(start, off, sz):
            return pltpu.make_async_copy(zero_ref.at[pl.ds(0, sz)],
                                         buf_ref.at[pl.ds(pl.multiple_of(start + off, SUBLANES), sz)], zsem)

        for act in (lambda c: c.start(), lambda c: c.wait()):
            def body(e, _, act=act):
                start = tail_ref[0, e]
                _pow2_pieces(tail_ref[1, e], blk // 2, lambda off, sz: act(zero_copy(start, off, sz)))
                return 0

            lax.fori_loop(0, N_EXPERTS, body, 0)

            def unused(j, _, act=act):
                act(zero_copy(tail_ref[0, N_EXPERTS], j * (blk // 2), blk // 2))
                return 0

            lax.fori_loop(0, tail_ref[1, N_EXPERTS] // (blk // 2), unused, 0)

    rio = lax.broadcasted_iota(I32, (n_sorted, td), 0)
    hit = rio == loc_ref[0:1, :]
    for k in range(1, TOP_K):
        hit = hit | (rio == loc_ref[k:k + 1, :])
    srt_ref[slot] = jnp.dot(hit.astype(BF16), h_ref[...], preferred_element_type=F32)

    def make_copy(s):
        return lambda loc0, glob0, sz: pltpu.make_async_copy(
            srt_ref.at[s, pl.ds(loc0, sz)], buf_ref.at[pl.ds(glob0, sz)], sem.at[s])

    _start_groups(meta_ref, td, make_copy(slot))

    @pl.when(i > 0)
    def _():
        _wait_groups(mprev_ref, n_sorted, lambda sz: make_copy(1 - slot)(0, 0, sz))

    @pl.when(i == pl.num_programs(0) - 1)
    def _():
        _wait_groups(meta_ref, n_sorted, lambda sz: make_copy(slot)(0, 0, sz))


def _dispatch(h2, loc, meta, tail, n_rows, td, blk):
    n = h2.shape[0]
    n_sorted = TOP_K * td + SUBLANES * N_EXPERTS
    return pl.pallas_call(
        functools.partial(_dispatch_kernel, td=td, blk=blk),
        grid=(n // td,),
        in_specs=[pl.BlockSpec((None, 1, META_LEN), lambda i: (i, 0, 0), memory_space=pltpu.SMEM),
                  pl.BlockSpec((None, 1, META_LEN), lambda i: (jnp.maximum(i - 1, 0), 0, 0),
                               memory_space=pltpu.SMEM),
                  pl.BlockSpec(memory_space=pltpu.SMEM),
                  pl.BlockSpec((TOP_K, td), lambda i: (0, i)),
                  pl.BlockSpec((td, D_MODEL), lambda i: (i, 0))],
        out_specs=pl.BlockSpec(memory_space=pl.ANY),
        out_shape=jax.ShapeDtypeStruct((n_rows, D_MODEL), F32),
        scratch_shapes=[pltpu.VMEM((2, n_sorted, D_MODEL), F32), pltpu.VMEM((blk // 2, D_MODEL), F32),
                        pltpu.SemaphoreType.DMA((2,)), pltpu.SemaphoreType.DMA],
        compiler_params=_cparams(1),
        name="dispatch",
    )(meta, meta, tail, loc, h2)


def _expert_kernel(be_ref, nu_ref, x_ref, wgu_ref, bgu_ref, wd_ref, bd_ref, y_ref, wgu_bf, wd_bf):
    i = pl.program_id(0)

    @pl.when(i < nu_ref[0])
    def _():
        @pl.when((i == 0) | (be_ref[i] != be_ref[jnp.maximum(i - 1, 0)]))
        def _():
            wgu_bf[...] = wgu_ref[...].astype(BF16)
            wd_bf[...] = wd_ref[...].astype(BF16)

        gu = jnp.dot(x_ref[...].astype(BF16), wgu_bf[...], preferred_element_type=F32) + bgu_ref[...]
        gate = jnp.minimum(gu[:, :D_FF], SWIGLU_LIMIT)
        up = jnp.clip(gu[:, D_FF:], -SWIGLU_LIMIT, SWIGLU_LIMIT)
        act = gate * _sigmoid(SWIGLU_ALPHA * gate) * (up + 1.0)
        y_ref[...] = jnp.dot(act.astype(BF16), wd_bf[...], preferred_element_type=F32) + bd_ref[...]

    @pl.when(i >= nu_ref[0])
    def _():
        y_ref[...] = jnp.zeros_like(y_ref)


def _experts(buf, blk_expert, n_used, wgu, bgu, wd, bd, blk):
    n_rows = buf.shape[0]
    used = lambda i, be, nu: (jnp.minimum(i, nu[0] - 1), 0)
    grid_spec = pltpu.PrefetchScalarGridSpec(
        num_scalar_prefetch=2,
        grid=(n_rows // blk,),
        in_specs=[pl.BlockSpec((blk, D_MODEL), used),
                  pl.BlockSpec((None, D_MODEL, 2 * D_FF), lambda i, be, nu: (be[i], 0, 0)),
                  pl.BlockSpec((None, 1, 2 * D_FF), lambda i, be, nu: (be[i], 0, 0)),
                  pl.BlockSpec((None, D_FF, D_MODEL), lambda i, be, nu: (be[i], 0, 0)),
                  pl.BlockSpec((None, 1, D_MODEL), lambda i, be, nu: (be[i], 0, 0))],
        out_specs=pl.BlockSpec((blk, D_MODEL), lambda i, be, nu: (i, 0)),
        scratch_shapes=[pltpu.VMEM((D_MODEL, 2 * D_FF), BF16), pltpu.VMEM((D_FF, D_MODEL), BF16)],
    )
    return pl.pallas_call(
        _expert_kernel,
        grid_spec=grid_spec,
        out_shape=jax.ShapeDtypeStruct((n_rows, D_MODEL), F32),
        compiler_params=_cparams(1),
        name="experts",
    )(blk_expert, n_used, buf, wgu, bgu.reshape(N_EXPERTS, 1, 2 * D_FF), wd, bd.reshape(N_EXPERTS, 1, D_MODEL))


def _combine_kernel(meta_ref, mnext_ref, loc_ref, gate_ref, x1_ref, mod_ref, gpost_ref, ybuf_ref, o_ref,
                    ys_ref, sem, *, td, nseg):
    i = pl.program_id(0)
    slot = i % 2
    n_sorted = ys_ref.shape[1]

    def make_copy(s):
        return lambda loc0, glob0, sz: pltpu.make_async_copy(
            ybuf_ref.at[pl.ds(glob0, sz)], ys_ref.at[s, pl.ds(loc0, sz)], sem.at[s])

    @pl.when(i == 0)
    def _():
        ys_ref[...] = jnp.zeros_like(ys_ref)
        _start_groups(meta_ref, td, make_copy(0))

    @pl.when(i + 1 < pl.num_programs(0))
    def _():
        _start_groups(mnext_ref, td, make_copy(1 - slot))

    _wait_groups(meta_ref, n_sorted, lambda sz: make_copy(slot)(0, 0, sz))
    y = ys_ref[slot].astype(BF16)
    jio = lax.broadcasted_iota(I32, (td, n_sorted), 1)
    w = jnp.zeros((td, n_sorted), F32)
    for k in range(TOP_K):
        w = w + jnp.where(jio == loc_ref[:, k:k + 1], gate_ref[:, k:k + 1], 0.0)
    w_hi = w.astype(BF16)
    w_lo = (w - w_hi.astype(F32)).astype(BF16)
    mo = jnp.dot(w_hi, y, preferred_element_type=F32) + jnp.dot(w_lo, y, preferred_element_type=F32)
    if nseg == 1:
        gate2 = mod_ref[0, 5:6, :]
    else:
        gate2 = jnp.concatenate([jnp.broadcast_to(mod_ref[s, 5:6, :], (td // nseg, D_MODEL)) for s in range(nseg)],
                                axis=0)
    o_ref[...] = x1_ref[...] + gate2 * (_rms(mo) * gpost_ref[...])


def _combine(meta, loc_t, gates_t, x1, ada3, b_off, seq, g_post, ybuf, td):
    n = x1.shape[0]
    n_tiles = n // td
    n_sorted = TOP_K * td + SUBLANES * N_EXPERTS
    if td >= seq:
        nseg = td // seq
        assert b_off % nseg == 0
        mod_idx = lambda i: (i + b_off // nseg, 0, 0)
    else:
        nseg = 1
        tpb = seq // td
        mod_idx = lambda i: (i // tpb + b_off, 0, 0)
    return pl.pallas_call(
        functools.partial(_combine_kernel, td=td, nseg=nseg),
        grid=(n_tiles,),
        in_specs=[pl.BlockSpec((None, 1, META_LEN), lambda i: (i, 0, 0), memory_space=pltpu.SMEM),
                  pl.BlockSpec((None, 1, META_LEN), lambda i: (jnp.minimum(i + 1, n_tiles - 1), 0, 0),
                               memory_space=pltpu.SMEM),
                  pl.BlockSpec((td, TOP_K), lambda i: (i, 0)),
                  pl.BlockSpec((td, TOP_K), lambda i: (i, 0)),
                  pl.BlockSpec((td, D_MODEL), lambda i: (i, 0)),
                  pl.BlockSpec((nseg, 6, D_MODEL), mod_idx),
                  pl.BlockSpec((1, D_MODEL), lambda i: (0, 0)),
                  pl.BlockSpec(memory_space=pl.ANY)],
        out_specs=pl.BlockSpec((td, D_MODEL), lambda i: (i, 0)),
        out_shape=jax.ShapeDtypeStruct((n, D_MODEL), F32),
        scratch_shapes=[pltpu.VMEM((2, n_sorted, D_MODEL), F32), pltpu.SemaphoreType.DMA((2,))],
        compiler_params=_cparams(1),
        name="combine",
    )(meta, meta, loc_t, gates_t, x1, ada3, g_post.reshape(1, D_MODEL), ybuf)


def _tile_sizes(nb, seq):
    n = nb * seq
    big = n >= 8192
    tmm = 512 if seq >= 512 else n
    return dict(
        tm=min(seq, 512),
        tmm=tmm,
        tq=min(seq, 512),
        tc=min(seq, 512),
        td=min(tmm, 256),
        blk=512 if big else 128,
    )


def _rope_tables(pos):
    half = HEAD_DIM // 2
    inv_freq = ROPE_THETA ** (-jnp.arange(half, dtype=F32) / half)
    ang = pos.astype(F32)[:, None] * inv_freq[None, :]
    cos, sin = jnp.cos(ang), jnp.sin(ang)
    reps = LANES // HEAD_DIM
    return (jnp.tile(jnp.concatenate([cos, cos], axis=1), (1, reps)),
            jnp.tile(jnp.concatenate([-sin, sin], axis=1), (1, reps)))


def _route_plan(cnt8, n_assign, blk):
    n_tiles = cnt8.shape[1]
    n_blocks = -(-(n_assign + n_tiles * N_EXPERTS * (SUBLANES - 1) + N_EXPERTS * (blk - 1)) // blk)
    tot = jnp.sum(cnt8, axis=1)
    padded = (tot + blk - 1) // blk * blk
    pend = jnp.cumsum(padded)
    pstart = pend - padded
    glob0 = pstart[:, None] + jnp.cumsum(cnt8, axis=1) - cnt8
    loc0 = jnp.cumsum(cnt8, axis=0) - cnt8
    meta = jnp.concatenate([loc0.T, cnt8.T, glob0.T, jnp.sum(cnt8, axis=0)[:, None]],
                           axis=1).reshape(n_tiles, 1, META_LEN)
    blk_row = jnp.arange(n_blocks, dtype=I32) * blk
    blk_expert = jnp.minimum(jnp.sum((pend[None, :] <= blk_row[:, None]).astype(I32), axis=1), N_EXPERTS - 1)
    n_used = (pend[-1:] // blk).astype(I32)
    n_rows = n_blocks * blk
    tail = jnp.stack([jnp.concatenate([pstart + tot, pend[-1:]]),
                      jnp.concatenate([padded - tot, n_rows - pend[-1:]])])
    return meta.astype(I32), tail.astype(I32), blk_expert, n_used, n_rows


def _layer(x, ada3, b_off, pos, k_past, v_past, s0, wts):
    nb, seq, _ = x.shape
    n = nb * seq
    ts = _tile_sizes(nb, seq)
    x2 = x.reshape(n, D_MODEL)
    cos_t, sin_t = _rope_tables(pos)
    q, k, v, hg, z = _proj(x2, ada3, b_off, seq, wts['g_pre_mix'], wts['w_in'], cos_t, sin_t, ts['tm'])
    k3 = k.reshape(nb, seq, KV_WIDTH)
    v3 = v.reshape(nb, seq, KV_WIDTH)
    if k_past is None:
        koff = 0
    else:
        rows = k_past.shape[1]
        koff = rows
        k3 = jnp.concatenate([k_past.reshape(nb, rows, KV_WIDTH), k3], axis=1)
        v3 = jnp.concatenate([v_past.reshape(nb, rows, KV_WIDTH), v3], axis=1)
    y_att = _attn(q.reshape(nb, seq, ATT_WIDTH), k3, v3, wts['sinks'], ts['tq'], koff)
    y_hg, s_new = _hgrn(hg.reshape(nb, seq, 4 * HG_WIDTH), wts['hgrn_lb'], wts['g_hgrn'], s0, ts['tc'])
    td, blk = ts['td'], ts['blk']
    x1, h2, gates, loc, cnt = _merge(
        y_att.reshape(n, ATT_WIDTH), y_hg.reshape(n, HG_WIDTH), z, x2, ada3, b_off, seq,
        wts['w_br_attn'], wts['w_br_hgrn'], wts['w_out'], wts['g_post_mix'], wts['g_pre_ffn'],
        wts['w_router_hi'], wts['w_router_lo'], wts['b_router'], ts['tmm'], td)
    meta, tail, blk_expert, n_used, n_rows = _route_plan(cnt.astype(I32), n * TOP_K, blk)
    buf = _dispatch(h2, loc, meta, tail, n_rows, td, blk)
    ybuf = _experts(buf, blk_expert, n_used, wts['w_gate_up'], wts['b_gate_up'], wts['w_down'], wts['b_down'], blk)
    out = _combine(meta, loc.T, gates.T, x1, ada3, b_off, seq, wts['g_post_ffn'], ybuf, td)
    win = min(WINDOW, k3.shape[1])
    k_new = k3[:, k3.shape[1] - win:].reshape(nb, win, N_KV_HEADS, HEAD_DIM)
    v_new = v3[:, v3.shape[1] - win:].reshape(nb, win, N_KV_HEADS, HEAD_DIM)
    return out.reshape(nb, seq, D_MODEL), k_new, v_new, s_new


def kernel(x_prompt, x_sample, cache_k, cache_v, state_hgrn, c_prompt, c_sample, w_ada, b_ada, g_pre_mix, g_post_mix, g_pre_ffn, g_post_ffn, w_in, attn_sinks, hgrn_lb, g_hgrn, w_br_attn, w_br_hgrn, w_out, w_router, b_router, w_gate_up, b_gate_up, w_down, b_down):
    n_bp = x_prompt.shape[0]
    wr = jnp.pad(w_router[0], ((0, 0), (0, LANES - N_EXPERTS)))
    wr_hi = wr.astype(BF16)
    wr_lo = (wr - wr_hi.astype(F32)).astype(BF16)
    wts = dict(
        g_pre_mix=g_pre_mix[0], g_post_mix=g_post_mix[0], g_pre_ffn=g_pre_ffn[0], g_post_ffn=g_post_ffn[0],
        w_in=w_in[0].astype(BF16), sinks=attn_sinks[0], hgrn_lb=hgrn_lb, g_hgrn=g_hgrn[0],
        w_br_attn=w_br_attn[0].astype(BF16), w_br_hgrn=w_br_hgrn[0].astype(BF16), w_out=w_out[0].astype(BF16),
        w_router_hi=wr_hi, w_router_lo=wr_lo, b_router=b_router[0],
        w_gate_up=w_gate_up[0], b_gate_up=b_gate_up[0], w_down=w_down[0], b_down=b_down[0])
    ada = _ada(jnp.concatenate([c_prompt, c_sample], axis=0), w_ada[0], b_ada[0])
    ada3 = ada.reshape(ada.shape[0], 6, D_MODEL)
    pos_p = jnp.arange(x_prompt.shape[1])
    pos_s = PAST_LEN + jnp.arange(x_sample.shape[1])
    ys, ks, vs, ss = _layer(x_sample, ada3, n_bp, pos_s, cache_k[0], cache_v[0], state_hgrn[0], wts)
    yp, kp, vp, sp = _layer(x_prompt, ada3, 0, pos_p, None, None, None, wts)
    return (yp, ys, kp[None], vp[None], sp[None], ks[None], vs[None], ss[None])
```

```python
import functools

import jax
import jax.numpy as jnp
from jax import lax
from jax.experimental import pallas as pl
from jax.experimental.pallas import tpu as pltpu

F32 = jnp.float32
BF16 = jnp.bfloat16
I32 = jnp.int32

D_MODEL = 1024
PAST_LEN = 1024
CHUNK = 64
HEAD_DIM = 64
N_Q_HEADS = 8
N_KV_HEADS = 2
Q_PER_KV = N_Q_HEADS // N_KV_HEADS
ATT_WIDTH = N_Q_HEADS * HEAD_DIM
KV_WIDTH = N_KV_HEADS * HEAD_DIM
WINDOW = 128
ROPE_THETA = 10000.0
HG_HEADS = 4
HG_DK = 128
HG_DV = 128
HG_WIDTH = HG_HEADS * HG_DK
HG_BLOCK = 16
N_EXPERTS = 32
TOP_K = 4
D_FF = 1024
SWIGLU_LIMIT = 7.0
SWIGLU_ALPHA = 1.702
RMS_EPS = 1e-6
N_IN = ATT_WIDTH + 2 * KV_WIDTH + 4 * HG_WIDTH + 2 * D_MODEL
HG_OFF = ATT_WIDTH + 2 * KV_WIDTH
Z_OFF = HG_OFF + 4 * HG_WIDTH

SORT_CHUNK = 256
ATTN_PHASE_CHUNKS = 8
META_LEN = 3 * N_EXPERTS + 1
LANES = 128
SUBLANES = 8
VMEM_LIMIT = 56 * 1024 * 1024

_NT = (((1,), (1,)), ((), ()))
_TN = (((0,), (0,)), ((), ()))


def _sigmoid(x):
    return 1.0 / (1.0 + jnp.exp(-x))


def _rms(x):
    return x * lax.rsqrt(jnp.mean(x * x, axis=-1, keepdims=True) + RMS_EPS)


def _cparams(n_axes):
    return pltpu.CompilerParams(dimension_semantics=("arbitrary",) * n_axes,
                                vmem_limit_bytes=VMEM_LIMIT)


def _ada_kernel(c_ref, w_ref, b_ref, o_ref):
    c = c_ref[...]
    s = c * _sigmoid(c)
    o_ref[...] = jnp.dot(s, w_ref[...], precision=lax.Precision.HIGHEST,
                         preferred_element_type=F32) + b_ref[...]


def _ada(c, w_ada, b_ada):
    nb = c.shape[0]
    n_out = w_ada.shape[1]
    tn = D_MODEL
    return pl.pallas_call(
        _ada_kernel,
        grid=(n_out // tn,),
        in_specs=[pl.BlockSpec((nb, D_MODEL), lambda j: (0, 0)),
                  pl.BlockSpec((D_MODEL, tn), lambda j: (0, j)),
                  pl.BlockSpec((1, tn), lambda j: (0, j))],
        out_specs=pl.BlockSpec((nb, tn), lambda j: (0, j)),
        out_shape=jax.ShapeDtypeStruct((nb, n_out), F32),
        compiler_params=_cparams(1),
        name="ada",
    )(c, w_ada, b_ada.reshape(1, n_out))


def _proj_kernel(x_ref, mod_ref, g_ref, w_ref, cos_ref, sin_ref,
                 q_ref, k_ref, v_ref, hg_ref, z_ref):
    x = x_ref[...]
    h = _rms(x) * g_ref[...]
    h = h * (1.0 + mod_ref[1:2, :]) + mod_ref[0:1, :]
    hb = h.astype(BF16)
    cos = cos_ref[...]
    sin = sin_ref[...]
    lane = lax.broadcasted_iota(I32, cos.shape, 1)
    first_half = (lane % HEAD_DIM) < (HEAD_DIM // 2)

    def cols(start, width):
        return jnp.dot(hb, w_ref[:, start:start + width], preferred_element_type=F32)

    def rope(p):
        partner = jnp.where(first_half, pltpu.roll(p, LANES - HEAD_DIM // 2, 1),
                            pltpu.roll(p, HEAD_DIM // 2, 1))
        return p * cos + partner * sin

    for j in range(ATT_WIDTH // LANES):
        q = rope(cols(j * LANES, LANES)) * (HEAD_DIM ** -0.5)
        q_ref[:, j * LANES:(j + 1) * LANES] = q.astype(BF16)
    k_ref[...] = rope(cols(ATT_WIDTH, KV_WIDTH))
    v_ref[...] = cols(ATT_WIDTH + KV_WIDTH, KV_WIDTH)
    for j in range(4):
        hg_ref[:, j * HG_WIDTH:(j + 1) * HG_WIDTH] = cols(HG_OFF + j * HG_WIDTH, HG_WIDTH)
    for j in range(4):
        z_ref[:, j * 512:(j + 1) * 512] = cols(Z_OFF + j * 512, 512)


def _proj(x2, ada3, b_off, seq, g_pre, w_in_bf, cos_t, sin_t, tm):
    n = x2.shape[0]
    tpb = seq // tm
    row = lambda i: (i, 0)
    return pl.pallas_call(
        _proj_kernel,
        grid=(n // tm,),
        in_specs=[pl.BlockSpec((tm, D_MODEL), row),
                  pl.BlockSpec((None, 6, D_MODEL), lambda i: (i // tpb + b_off, 0, 0)),
                  pl.BlockSpec((1, D_MODEL), lambda i: (0, 0)),
                  pl.BlockSpec((D_MODEL, N_IN), lambda i: (0, 0)),
                  pl.BlockSpec((tm, LANES), lambda i: (i % tpb, 0)),
                  pl.BlockSpec((tm, LANES), lambda i: (i % tpb, 0))],
        out_specs=[pl.BlockSpec((tm, ATT_WIDTH), row),
                   pl.BlockSpec((tm, KV_WIDTH), row),
                   pl.BlockSpec((tm, KV_WIDTH), row),
                   pl.BlockSpec((tm, 4 * HG_WIDTH), row),
                   pl.BlockSpec((tm, 2 * D_MODEL), row)],
        out_shape=[jax.ShapeDtypeStruct((n, ATT_WIDTH), BF16),
                   jax.ShapeDtypeStruct((n, KV_WIDTH), F32),
                   jax.ShapeDtypeStruct((n, KV_WIDTH), F32),
                   jax.ShapeDtypeStruct((n, 4 * HG_WIDTH), F32),
                   jax.ShapeDtypeStruct((n, 2 * D_MODEL), F32)],
        compiler_params=_cparams(1),
        name="proj",
    )(x2, ada3, g_pre.reshape(1, D_MODEL), w_in_bf, cos_t, sin_t)


def _attn_kernel(sink_ref, q_ref, k_ref, v_ref, o_ref, *, tq, koff):
    t = pl.program_id(1)
    span = WINDOW + CHUNK
    lane = lax.broadcasted_iota(I32, (span, KV_WIDTH), 1)
    top_row = lax.broadcasted_iota(I32, (2 * CHUNK, 1), 0) < CHUNK
    n_chunks = tq // CHUNK
    for c0 in range(0, n_chunks, ATTN_PHASE_CHUNKS):
        units = []
        for c in range(c0, min(c0 + ATTN_PHASE_CHUNKS, n_chunks)):
            r = t * tq + c * CHUNK + koff
            ks = pl.multiple_of(jnp.maximum(r - WINDOW, 0), CHUNK)
            kf = k_ref[pl.ds(ks, span), :]
            vf = v_ref[pl.ds(ks, span), :]
            valid = ks + lax.broadcasted_iota(I32, (1, span), 1) < r + CHUNK
            rows = slice(c * CHUNK, (c + 1) * CHUNK)
            for g in range(N_KV_HEADS):
                own = (lane // HEAD_DIM) == g
                k_own = jnp.where(own, kf, 0.0)
                v_own = jnp.where(own, vf, 0.0)
                k_oth = pltpu.roll(k_own, HEAD_DIM, 1)
                v_oth = pltpu.roll(v_own, HEAD_DIM, 1)
                k_par = (k_own, k_oth) if g == 0 else (k_oth, k_own)
                v_par = (v_own, v_oth) if g == 0 else (v_oth, v_own)
                qq = jnp.concatenate([q_ref[rows, (2 * g + i) * LANES:(2 * g + i + 1) * LANES] for i in range(2)],
                                     axis=0)
                for par in range(2):
                    h_top = Q_PER_KV * g + par
                    units.append((qq, k_par[par].astype(BF16), v_par[par].astype(BF16),
                                  jnp.where(top_row, sink_ref[h_top], sink_ref[h_top + 2]), valid))
        ss = [jnp.where(u[4], lax.dot_general(u[0], u[1], _NT, preferred_element_type=F32), -jnp.inf) for u in units]
        ms = [jnp.maximum(jnp.max(s, axis=-1, keepdims=True), u[3]) for s, u in zip(ss, units)]
        ps = [jnp.exp(s - m) for s, m in zip(ss, ms)]
        ds = [jnp.sum(p, axis=-1, keepdims=True) + jnp.exp(u[3] - m) for p, m, u in zip(ps, ms, units)]
        os_ = [jnp.dot((p / d).astype(BF16), u[2], preferred_element_type=F32) for p, d, u in zip(ps, ds, units)]
        for ci, c in enumerate(range(c0, min(c0 + ATTN_PHASE_CHUNKS, n_chunks))):
            rows = slice(c * CHUNK, (c + 1) * CHUNK)
            for g in range(N_KV_HEADS):
                u0 = (ci * N_KV_HEADS + g) * 2
                acc = os_[u0] + os_[u0 + 1]
                for i in range(2):
                    o_ref[rows, (2 * g + i) * LANES:(2 * g + i + 1) * LANES] = (
                        acc[i * CHUNK:(i + 1) * CHUNK].astype(BF16))


def _attn(q3, k3, v3, sinks, tq, koff):
    nb, tl, _ = q3.shape
    tk = k3.shape[1]
    return pl.pallas_call(
        functools.partial(_attn_kernel, tq=tq, koff=koff),
        grid=(nb, tl // tq),
        in_specs=[pl.BlockSpec(memory_space=pltpu.SMEM),
                  pl.BlockSpec((None, tq, ATT_WIDTH), lambda b, t: (b, t, 0)),
                  pl.BlockSpec((None, tk, KV_WIDTH), lambda b, t: (b, 0, 0)),
                  pl.BlockSpec((None, tk, KV_WIDTH), lambda b, t: (b, 0, 0))],
        out_specs=pl.BlockSpec((None, tq, ATT_WIDTH), lambda b, t: (b, t, 0)),
        out_shape=jax.ShapeDtypeStruct((nb, tl, ATT_WIDTH), BF16),
        compiler_params=_cparams(2),
        name="attn",
    )(sinks, q3, k3, v3)


def _hgrn_kernel(*refs, tc, has_s0):
    if has_s0:
        hg_ref, lbp_ref, gh_ref, s0_ref, y_ref, s_out_ref, st_ref = refs
    else:
        hg_ref, lbp_ref, gh_ref, y_ref, s_out_ref, st_ref = refs
        s0_ref = None
    t = pl.program_id(1)

    @pl.when(t == 0)
    def _():
        for h in range(HG_HEADS):
            st_ref[h] = s0_ref[h].T if has_s0 else jnp.zeros((HG_DV, HG_DK), F32)

    l0 = lbp_ref[0:1, :]
    l1 = lbp_ref[1:2, :]
    lm = jnp.maximum(l0, l1)
    e0 = jnp.exp(l0 - lm)
    lb = e0 / (e0 + jnp.exp(l1 - lm))

    hq = hg_ref[:, 0:HG_WIDTH]
    hf = hg_ref[:, HG_WIDTH:2 * HG_WIDTH]
    hi = hg_ref[:, 2 * HG_WIDTH:3 * HG_WIDTH]
    hz = hg_ref[:, 3 * HG_WIDTH:4 * HG_WIDTH]
    f = lb + (1.0 - lb) * _sigmoid(hf)
    q = hq * _sigmoid(hq)
    kk = 1.0 - f
    g = jnp.log(f)

    r16 = lax.broadcasted_iota(I32, g.shape, 0) % HG_BLOCK
    b = g
    suf = g
    for s in (1, 2, 4, 8):
        b = b + jnp.where(r16 >= s, pltpu.roll(b, s, 0), 0.0)
        suf = suf + jnp.where(r16 < HG_BLOCK - s, pltpu.roll(suf, tc - s, 0), 0.0)
    qt = (q * jnp.exp(b)).astype(BF16)
    kt = (kk * jnp.exp(-b)).astype(BF16)
    kd = (kk * jnp.exp(suf - g)).astype(BF16)
    vb = hi.astype(BF16)
    grp = min(tc, LANES)
    gi = lax.broadcasted_iota(I32, (grp, grp), 0)
    gj = lax.broadcasted_iota(I32, (grp, grp), 1)
    keep = ((gi // HG_BLOCK) == (gj // HG_BLOCK)) & (gi >= gj)
    heads = [slice(h * HG_DK, (h + 1) * HG_DK) for h in range(HG_HEADS)]
    o_groups = [[] for _ in heads]
    for g in range(tc // grp):
        gs = slice(g * grp, (g + 1) * grp)
        a_s = [jnp.where(keep, lax.dot_general(qt[gs, cs], kt[gs, cs], _NT, preferred_element_type=F32), 0.0)
               for cs in heads]
        o_intra = [jnp.dot(a.astype(BF16), vb[gs, cs], preferred_element_type=F32) for a, cs in zip(a_s, heads)]
        o_inter = [[] for _ in heads]
        for j in range(grp // HG_BLOCK):
            r0 = g * grp + j * HG_BLOCK
            rs = slice(r0, r0 + HG_BLOCK)
            sts = [st_ref[h] for h in range(HG_HEADS)]
            ups = [lax.dot_general(vb[rs, cs], kd[rs, cs], _TN, preferred_element_type=F32) for cs in heads]
            for h, cs in enumerate(heads):
                o_inter[h].append(lax.dot_general(qt[rs, cs], sts[h].astype(BF16), _NT, preferred_element_type=F32))
            for h, cs in enumerate(heads):
                dec = jnp.exp(b[r0 + HG_BLOCK - 1:r0 + HG_BLOCK, cs])
                st_ref[h] = sts[h] * dec + ups[h]
        for h in range(HG_HEADS):
            o_groups[h].append(o_intra[h] + jnp.concatenate(o_inter[h], axis=0))
    o_heads = []
    for h, cs in enumerate(heads):
        o_h = jnp.concatenate(o_groups[h], axis=0)
        zs = hz[:, cs]
        o_heads.append(_rms(o_h) * gh_ref[:, cs] * (zs * _sigmoid(zs)))
    y_ref[...] = jnp.concatenate(o_heads, axis=1).astype(BF16)

    @pl.when(t == pl.num_programs(1) - 1)
    def _():
        for h in range(HG_HEADS):
            s_out_ref[h] = st_ref[h].T


def _hgrn(hg3, hgrn_lb, g_hgrn, s0, tc):
    nb, tl, _ = hg3.shape
    has_s0 = s0 is not None
    st_spec = pl.BlockSpec((None, HG_HEADS, HG_DK, HG_DV), lambda b, t: (b, 0, 0, 0))
    in_specs = [pl.BlockSpec((None, tc, 4 * HG_WIDTH), lambda b, t: (b, t, 0)),
                pl.BlockSpec((2, HG_WIDTH), lambda b, t: (0, 0)),
                pl.BlockSpec((1, HG_WIDTH), lambda b, t: (0, 0))]
    args = [hg3, hgrn_lb, g_hgrn.reshape(1, HG_WIDTH)]
    if has_s0:
        in_specs.append(st_spec)
        args.append(s0)
    return pl.pallas_call(
        functools.partial(_hgrn_kernel, tc=tc, has_s0=has_s0),
        grid=(nb, tl // tc),
        in_specs=in_specs,
        out_specs=[pl.BlockSpec((None, tc, HG_WIDTH), lambda b, t: (b, t, 0)), st_spec],
        out_shape=[jax.ShapeDtypeStruct((nb, tl, HG_WIDTH), BF16),
                   jax.ShapeDtypeStruct((nb, HG_HEADS, HG_DK, HG_DV), F32)],
        scratch_shapes=[pltpu.VMEM((HG_HEADS, HG_DV, HG_DK), F32)],
        compiler_params=_cparams(2),
        name="hgrn",
    )(*args)


def _merge_kernel(a_ref, b_ref, z_ref, x_ref, mod_ref, wa_ref, wb_ref, wo_ref, gpost_ref, gpre_ref,
                  wrh_ref, wrl_ref, br_ref, x1_ref, h2_ref, gate_ref, loc_ref, cnt_ref, *, tm, nseg, sub):
    i = pl.program_id(0)
    seg = tm // nseg

    def mod_rows(r, part):
        if nseg == 1:
            return mod_ref[0, r:r + 1, :]
        segs = range(part * sub // seg, (part + 1) * sub // seg)
        return jnp.concatenate([jnp.broadcast_to(mod_ref[s, r:r + 1, :], (seg, D_MODEL)) for s in segs], axis=0)

    @pl.when(i == 0)
    def _():
        cnt_ref[...] = jnp.zeros_like(cnt_ref)

    earlier = (lax.broadcasted_iota(I32, (sub, sub), 0) < lax.broadcasted_iota(I32, (sub, sub), 1)).astype(BF16)
    lower = (lax.broadcasted_iota(I32, (N_EXPERTS, N_EXPERTS), 1)
             < lax.broadcasted_iota(I32, (N_EXPERTS, N_EXPERTS), 0)).astype(BF16)
    eio = lax.broadcasted_iota(I32, (N_EXPERTS, sub), 0).astype(F32)
    tile_lane = lax.broadcasted_iota(I32, cnt_ref.shape, 1)
    parts = range(tm // sub)
    rss = [slice(part * sub, (part + 1) * sub) for part in parts]
    ms = [(_sigmoid(z_ref[rs, 0:D_MODEL]) * jnp.dot(a_ref[rs, :], wa_ref[...], preferred_element_type=F32)
           + _sigmoid(z_ref[rs, D_MODEL:2 * D_MODEL]) * jnp.dot(b_ref[rs, :], wb_ref[...], preferred_element_type=F32))
          for rs in rss]
    ys = [jnp.dot(m.astype(BF16), wo_ref[...], preferred_element_type=F32) for m in ms]
    x1s = [x_ref[rs, :] + mod_rows(2, part) * (_rms(y) * gpost_ref[...]) for part, rs, y in zip(parts, rss, ys)]
    for rs, x1 in zip(rss, x1s):
        x1_ref[rs, :] = x1
    h2s = [(_rms(x1) * gpre_ref[...]) * (1.0 + mod_rows(4, part)) + mod_rows(3, part) for part, x1 in zip(parts, x1s)]
    h_his = [h2.astype(BF16) for h2 in h2s]
    for rs, h_hi in zip(rss, h_his):
        h2_ref[rs, :] = h_hi

    h_los = [(h2 - h_hi.astype(F32)).astype(BF16) for h2, h_hi in zip(h2s, h_his)]
    lgs = [(jnp.dot(h_hi, wrh_ref[...], preferred_element_type=F32)
            + jnp.dot(h_hi, wrl_ref[...], preferred_element_type=F32)
            + jnp.dot(h_lo, wrh_ref[...], preferred_element_type=F32)) for h_hi, h_lo in zip(h_his, h_los)]
    curs = [lg.T[0:N_EXPERTS, :] + br_ref[...] for lg in lgs]
    vals = [[] for _ in parts]
    sels = [[] for _ in parts]
    for k in range(TOP_K):
        for part in parts:
            cur = curs[part]
            mx = jnp.max(cur, axis=0, keepdims=True)
            ik = jnp.min(jnp.where(cur == mx, eio, float(N_EXPERTS)), axis=0, keepdims=True)
            sel = eio == ik
            vals[part].append(mx)
            sels[part].append(sel)
            curs[part] = jnp.where(sel, -jnp.inf, cur)
    for part, rs in zip(parts, rss):
        es = [jnp.exp(v - vals[part][0]) for v in vals[part]]
        den = es[0] + es[1] + es[2] + es[3]
        for k in range(TOP_K):
            gate_ref[k:k + 1, rs] = es[k] / den

    for part, rs in zip(parts, rss):
        onehot = jnp.zeros((N_EXPERTS, sub), F32)
        for k in range(TOP_K):
            onehot = onehot + sels[part][k].astype(F32)
        n8 = jnp.ceil(jnp.sum(onehot, axis=1, keepdims=True) * 0.125) * 8.0
        base = (jnp.dot(lower, jnp.broadcast_to(n8, onehot.shape).astype(BF16), preferred_element_type=F32)
                + jnp.dot(onehot.astype(BF16), earlier, preferred_element_type=F32))
        for k in range(TOP_K):
            loc_ref[k:k + 1, rs] = jnp.sum(jnp.where(sels[part][k], base, 0.0), axis=0, keepdims=True).astype(I32)
        cnt_ref[...] = jnp.where(tile_lane == i * (tm // sub) + part, n8, cnt_ref[...])


def _merge(a2, b2, z2, x2, ada3, b_off, seq, wa, wb, wo, g_post, g_pre, wr_hi, wr_lo, br, tm, sub):
    n = x2.shape[0]
    if tm >= seq:
        nseg = tm // seq
        assert b_off % nseg == 0
        mod_idx = lambda i: (i + b_off // nseg, 0, 0)
    else:
        nseg = 1
        tpb = seq // tm
        mod_idx = lambda i: (i // tpb + b_off, 0, 0)
    row = lambda i: (i, 0)
    const = lambda i: (0, 0)
    col = lambda i: (0, i)
    return pl.pallas_call(
        functools.partial(_merge_kernel, tm=tm, nseg=nseg, sub=sub),
        grid=(n // tm,),
        in_specs=[pl.BlockSpec((tm, ATT_WIDTH), row),
                  pl.BlockSpec((tm, HG_WIDTH), row),
                  pl.BlockSpec((tm, 2 * D_MODEL), row),
                  pl.BlockSpec((tm, D_MODEL), row),
                  pl.BlockSpec((nseg, 6, D_MODEL), mod_idx),
                  pl.BlockSpec((ATT_WIDTH, D_MODEL), const),
                  pl.BlockSpec((HG_WIDTH, D_MODEL), const),
                  pl.BlockSpec((D_MODEL, D_MODEL), const),
                  pl.BlockSpec((1, D_MODEL), const),
                  pl.BlockSpec((1, D_MODEL), const),
                  pl.BlockSpec((D_MODEL, LANES), const),
                  pl.BlockSpec((D_MODEL, LANES), const),
                  pl.BlockSpec((N_EXPERTS, 1), const)],
        out_specs=[pl.BlockSpec((tm, D_MODEL), row),
                   pl.BlockSpec((tm, D_MODEL), row),
                   pl.BlockSpec((TOP_K, tm), col),
                   pl.BlockSpec((TOP_K, tm), col),
                   pl.BlockSpec((N_EXPERTS, n // sub), const)],
        out_shape=[jax.ShapeDtypeStruct((n, D_MODEL), F32),
                   jax.ShapeDtypeStruct((n, D_MODEL), BF16),
                   jax.ShapeDtypeStruct((TOP_K, n), F32),
                   jax.ShapeDtypeStruct((TOP_K, n), I32),
                   jax.ShapeDtypeStruct((N_EXPERTS, n // sub), F32)],
        compiler_params=_cparams(1),
        name="merge",
    )(a2, b2, z2, x2, ada3, wa, wb, wo, g_post.reshape(1, D_MODEL), g_pre.reshape(1, D_MODEL),
      wr_hi, wr_lo, br.reshape(N_EXPERTS, 1))


def _pow2_pieces(n, max_piece, fn):
    sz = max_piece
    while sz >= SUBLANES:
        off = n & ~(2 * sz - 1)

        @pl.when((n & sz) != 0)
        def _(off=off, sz=sz):
            fn(off, sz)

        sz //= 2


def _start_groups(meta_ref, td, make_copy):
    def body(e, _):
        loc0 = meta_ref[0, e]
        n8 = meta_ref[0, N_EXPERTS + e]
        glob0 = meta_ref[0, 2 * N_EXPERTS + e]
        _pow2_pieces(n8, td, lambda off, sz: make_copy(pl.multiple_of(loc0 + off, SUBLANES),
                                                       pl.multiple_of(glob0 + off, SUBLANES), sz).start())
        return 0

    lax.fori_loop(0, N_EXPERTS, body, 0, unroll=2)


def _wait_groups(meta_ref, n_sorted, make_wait):
    max_piece = 1 << (n_sorted.bit_length() - 1)
    _pow2_pieces(meta_ref[0, 3 * N_EXPERTS], max_piece, lambda off, sz: make_wait(sz).wait())


def _dispatch_kernel(meta_ref, mprev_ref, tail_ref, loc_ref, h_ref, buf_ref, srt_ref, zero_ref, sem, zsem,
                     *, td, blk):
    i = pl.program_id(0)
    slot = i % 2
    n_sorted = srt_ref.shape[1]

    @pl.when(i == 0)
    def _():
        zero_ref[...] = jnp.zeros_like(zero_ref)

        def zero_copy(start, off, sz):
            return pltpu.make_async_copy(zero_ref.at[pl.ds(0, sz)],
                                         buf_ref.at[pl.ds(pl.multiple_of(start + off, SUBLANES), sz)], zsem)

        for act in (lambda c: c.start(), lambda c: c.wait()):
            def body(e, _, act=act):
                start = tail_ref[0, e]
                _pow2_pieces(tail_ref[1, e], blk // 2, lambda off, sz: act(zero_copy(start, off, sz)))
                return 0

            lax.fori_loop(0, N_EXPERTS, body, 0)

            def unused(j, _, act=act):
                act(zero_copy(tail_ref[0, N_EXPERTS], j * (blk // 2), blk // 2))
                return 0

            lax.fori_loop(0, tail_ref[1, N_EXPERTS] // (blk // 2), unused, 0)

    for j0 in range(0, n_sorted, SORT_CHUNK):
        rio = j0 + lax.broadcasted_iota(I32, (SORT_CHUNK, td), 0)
        hit = rio == loc_ref[0:1, :]
        for k in range(1, TOP_K):
            hit = hit | (rio == loc_ref[k:k + 1, :])
        srt_ref[slot, j0:j0 + SORT_CHUNK, :] = jnp.dot(hit.astype(BF16), h_ref[...], preferred_element_type=F32)

    def make_copy(s):
        return lambda loc0, glob0, sz: pltpu.make_async_copy(
            srt_ref.at[s, pl.ds(loc0, sz)], buf_ref.at[pl.ds(glob0, sz)], sem.at[s])

    _start_groups(meta_ref, td, make_copy(slot))

    @pl.when(i > 0)
    def _():
        _wait_groups(mprev_ref, n_sorted, lambda sz: make_copy(1 - slot)(0, 0, sz))

    @pl.when(i == pl.num_programs(0) - 1)
    def _():
        _wait_groups(meta_ref, n_sorted, lambda sz: make_copy(slot)(0, 0, sz))


def _dispatch(h2, loc, meta, tail, n_rows, td, blk):
    n = h2.shape[0]
    n_sorted = TOP_K * td + SUBLANES * N_EXPERTS
    return pl.pallas_call(
        functools.partial(_dispatch_kernel, td=td, blk=blk),
        grid=(n // td,),
        in_specs=[pl.BlockSpec((None, 1, META_LEN), lambda i: (i, 0, 0), memory_space=pltpu.SMEM),
                  pl.BlockSpec((None, 1, META_LEN), lambda i: (jnp.maximum(i - 1, 0), 0, 0),
                               memory_space=pltpu.SMEM),
                  pl.BlockSpec(memory_space=pltpu.SMEM),
                  pl.BlockSpec((TOP_K, td), lambda i: (0, i)),
                  pl.BlockSpec((td, D_MODEL), lambda i: (i, 0))],
        out_specs=pl.BlockSpec(memory_space=pl.ANY),
        out_shape=jax.ShapeDtypeStruct((n_rows, D_MODEL), F32),
        scratch_shapes=[pltpu.VMEM((2, n_sorted, D_MODEL), F32), pltpu.VMEM((blk // 2, D_MODEL), F32),
                        pltpu.SemaphoreType.DMA((2,)), pltpu.SemaphoreType.DMA],
        compiler_params=_cparams(1),
        name="dispatch",
    )(meta, meta, tail, loc, h2)


def _expert_kernel(be_ref, nu_ref, x_ref, wgu_ref, bgu_ref, wd_ref, bd_ref, y_ref, wgu_bf, wd_bf):
    i = pl.program_id(0)

    @pl.when(i < nu_ref[0])
    def _():
        @pl.when((i == 0) | (be_ref[i] != be_ref[jnp.maximum(i - 1, 0)]))
        def _():
            wgu_bf[...] = wgu_ref[...].astype(BF16)
            wd_bf[...] = wd_ref[...].astype(BF16)

        gu = jnp.dot(x_ref[...].astype(BF16), wgu_bf[...], preferred_element_type=F32) + bgu_ref[...]
        gate = jnp.minimum(gu[:, :D_FF], SWIGLU_LIMIT)
        up = jnp.clip(gu[:, D_FF:], -SWIGLU_LIMIT, SWIGLU_LIMIT)
        act = gate * _sigmoid(SWIGLU_ALPHA * gate) * (up + 1.0)
        y_ref[...] = jnp.dot(act.astype(BF16), wd_bf[...], preferred_element_type=F32) + bd_ref[...]

    @pl.when(i >= nu_ref[0])
    def _():
        y_ref[...] = jnp.zeros_like(y_ref)


def _experts(buf, blk_expert, n_used, wgu, bgu, wd, bd, blk):
    n_rows = buf.shape[0]
    used = lambda i, be, nu: (jnp.minimum(i, nu[0] - 1), 0)
    grid_spec = pltpu.PrefetchScalarGridSpec(
        num_scalar_prefetch=2,
        grid=(n_rows // blk,),
        in_specs=[pl.BlockSpec((blk, D_MODEL), used),
                  pl.BlockSpec((None, D_MODEL, 2 * D_FF), lambda i, be, nu: (be[i], 0, 0)),
                  pl.BlockSpec((None, 1, 2 * D_FF), lambda i, be, nu: (be[i], 0, 0)),
                  pl.BlockSpec((None, D_FF, D_MODEL), lambda i, be, nu: (be[i], 0, 0)),
                  pl.BlockSpec((None, 1, D_MODEL), lambda i, be, nu: (be[i], 0, 0))],
        out_specs=pl.BlockSpec((blk, D_MODEL), lambda i, be, nu: (i, 0)),
        scratch_shapes=[pltpu.VMEM((D_MODEL, 2 * D_FF), BF16), pltpu.VMEM((D_FF, D_MODEL), BF16)],
    )
    return pl.pallas_call(
        _expert_kernel,
        grid_spec=grid_spec,
        out_shape=jax.ShapeDtypeStruct((n_rows, D_MODEL), F32),
        compiler_params=_cparams(1),
        name="experts",
    )(blk_expert, n_used, buf, wgu, bgu.reshape(N_EXPERTS, 1, 2 * D_FF), wd, bd.reshape(N_EXPERTS, 1, D_MODEL))


def _combine_kernel(meta_ref, mnext_ref, loc_ref, gate_ref, x1_ref, mod_ref, gpost_ref, ybuf_ref, o_ref,
                    ys_ref, sem, *, td, nseg):
    i = pl.program_id(0)
    slot = i % 2
    n_sorted = ys_ref.shape[1]

    def make_copy(s):
        return lambda loc0, glob0, sz: pltpu.make_async_copy(
            ybuf_ref.at[pl.ds(glob0, sz)], ys_ref.at[s, pl.ds(loc0, sz)], sem.at[s])

    @pl.when(i == 0)
    def _():
        ys_ref[...] = jnp.zeros_like(ys_ref)
        _start_groups(meta_ref, td, make_copy(0))

    @pl.when(i + 1 < pl.num_programs(0))
    def _():
        _start_groups(mnext_ref, td, make_copy(1 - slot))

    _wait_groups(meta_ref, n_sorted, lambda sz: make_copy(slot)(0, 0, sz))
    mo = None
    for j0 in range(0, n_sorted, SORT_CHUNK):
        jio = j0 + lax.broadcasted_iota(I32, (td, SORT_CHUNK), 1)
        w = jnp.zeros((td, SORT_CHUNK), F32)
        for k in range(TOP_K):
            w = w + jnp.where(jio == loc_ref[:, k:k + 1], gate_ref[:, k:k + 1], 0.0)
        w_hi = w.astype(BF16)
        w_lo = (w - w_hi.astype(F32)).astype(BF16)
        y = ys_ref[slot, j0:j0 + SORT_CHUNK, :].astype(BF16)
        part = jnp.dot(w_hi, y, preferred_element_type=F32) + jnp.dot(w_lo, y, preferred_element_type=F32)
        mo = part if mo is None else mo + part
    if nseg == 1:
        gate2 = mod_ref[0, 5:6, :]
    else:
        gate2 = jnp.concatenate([jnp.broadcast_to(mod_ref[s, 5:6, :], (td // nseg, D_MODEL)) for s in range(nseg)],
                                axis=0)
    o_ref[...] = x1_ref[...] + gate2 * (_rms(mo) * gpost_ref[...])


def _combine(meta, loc_t, gates_t, x1, ada3, b_off, seq, g_post, ybuf, td):
    n = x1.shape[0]
    n_tiles = n // td
    n_sorted = TOP_K * td + SUBLANES * N_EXPERTS
    if td >= seq:
        nseg = td // seq
        assert b_off % nseg == 0
        mod_idx = lambda i: (i + b_off // nseg, 0, 0)
    else:
        nseg = 1
        tpb = seq // td
        mod_idx = lambda i: (i // tpb + b_off, 0, 0)
    return pl.pallas_call(
        functools.partial(_combine_kernel, td=td, nseg=nseg),
        grid=(n_tiles,),
        in_specs=[pl.BlockSpec((None, 1, META_LEN), lambda i: (i, 0, 0), memory_space=pltpu.SMEM),
                  pl.BlockSpec((None, 1, META_LEN), lambda i: (jnp.minimum(i + 1, n_tiles - 1), 0, 0),
                               memory_space=pltpu.SMEM),
                  pl.BlockSpec((td, TOP_K), lambda i: (i, 0)),
                  pl.BlockSpec((td, TOP_K), lambda i: (i, 0)),
                  pl.BlockSpec((td, D_MODEL), lambda i: (i, 0)),
                  pl.BlockSpec((nseg, 6, D_MODEL), mod_idx),
                  pl.BlockSpec((1, D_MODEL), lambda i: (0, 0)),
                  pl.BlockSpec(memory_space=pl.ANY)],
        out_specs=pl.BlockSpec((td, D_MODEL), lambda i: (i, 0)),
        out_shape=jax.ShapeDtypeStruct((n, D_MODEL), F32),
        scratch_shapes=[pltpu.VMEM((2, n_sorted, D_MODEL), F32), pltpu.SemaphoreType.DMA((2,))],
        compiler_params=_cparams(1),
        name="combine",
    )(meta, meta, loc_t, gates_t, x1, ada3, g_post.reshape(1, D_MODEL), ybuf)


def _tile_sizes(nb, seq):
    n = nb * seq
    big = n >= 8192
    tmm = 512 if seq >= 512 else n
    return dict(
        tm=min(seq, 512),
        tmm=tmm,
        tq=min(seq, 512),
        tc=min(seq, 512),
        td=min(tmm, 256),
        blk=512 if big else 128,
    )


def _rope_tables(pos):
    half = HEAD_DIM // 2
    inv_freq = ROPE_THETA ** (-jnp.arange(half, dtype=F32) / half)
    ang = pos.astype(F32)[:, None] * inv_freq[None, :]
    cos, sin = jnp.cos(ang), jnp.sin(ang)
    reps = LANES // HEAD_DIM
    return (jnp.tile(jnp.concatenate([cos, cos], axis=1), (1, reps)),
            jnp.tile(jnp.concatenate([-sin, sin], axis=1), (1, reps)))


def _route_plan(cnt8, n_assign, blk):
    n_tiles = cnt8.shape[1]
    n_blocks = -(-(n_assign + n_tiles * N_EXPERTS * (SUBLANES - 1) + N_EXPERTS * (blk - 1)) // blk)
    tot = jnp.sum(cnt8, axis=1)
    padded = (tot + blk - 1) // blk * blk
    pend = jnp.cumsum(padded)
    pstart = pend - padded
    glob0 = pstart[:, None] + jnp.cumsum(cnt8, axis=1) - cnt8
    loc0 = jnp.cumsum(cnt8, axis=0) - cnt8
    meta = jnp.concatenate([loc0.T, cnt8.T, glob0.T, jnp.sum(cnt8, axis=0)[:, None]],
                           axis=1).reshape(n_tiles, 1, META_LEN)
    blk_row = jnp.arange(n_blocks, dtype=I32) * blk
    blk_expert = jnp.minimum(jnp.sum((pend[None, :] <= blk_row[:, None]).astype(I32), axis=1), N_EXPERTS - 1)
    n_used = (pend[-1:] // blk).astype(I32)
    n_rows = n_blocks * blk
    tail = jnp.stack([jnp.concatenate([pstart + tot, pend[-1:]]),
                      jnp.concatenate([padded - tot, n_rows - pend[-1:]])])
    return meta.astype(I32), tail.astype(I32), blk_expert, n_used, n_rows


def _layer(x, ada3, b_off, pos, k_past, v_past, s0, wts):
    nb, seq, _ = x.shape
    n = nb * seq
    ts = _tile_sizes(nb, seq)
    x2 = x.reshape(n, D_MODEL)
    cos_t, sin_t = _rope_tables(pos)
    q, k, v, hg, z = _proj(x2, ada3, b_off, seq, wts['g_pre_mix'], wts['w_in'], cos_t, sin_t, ts['tm'])
    k3 = k.reshape(nb, seq, KV_WIDTH)
    v3 = v.reshape(nb, seq, KV_WIDTH)
    if k_past is None:
        koff = 0
    else:
        rows = k_past.shape[1]
        koff = rows
        k3 = jnp.concatenate([k_past.reshape(nb, rows, KV_WIDTH), k3], axis=1)
        v3 = jnp.concatenate([v_past.reshape(nb, rows, KV_WIDTH), v3], axis=1)
    y_att = _attn(q.reshape(nb, seq, ATT_WIDTH), k3, v3, wts['sinks'], ts['tq'], koff)
    y_hg, s_new = _hgrn(hg.reshape(nb, seq, 4 * HG_WIDTH), wts['hgrn_lb'], wts['g_hgrn'], s0, ts['tc'])
    td, blk = ts['td'], ts['blk']
    x1, h2, gates, loc, cnt = _merge(
        y_att.reshape(n, ATT_WIDTH), y_hg.reshape(n, HG_WIDTH), z, x2, ada3, b_off, seq,
        wts['w_br_attn'], wts['w_br_hgrn'], wts['w_out'], wts['g_post_mix'], wts['g_pre_ffn'],
        wts['w_router_hi'], wts['w_router_lo'], wts['b_router'], ts['tmm'], td)
    meta, tail, blk_expert, n_used, n_rows = _route_plan(cnt.astype(I32), n * TOP_K, blk)
    buf = _dispatch(h2, loc, meta, tail, n_rows, td, blk)
    ybuf = _experts(buf, blk_expert, n_used, wts['w_gate_up'], wts['b_gate_up'], wts['w_down'], wts['b_down'], blk)
    out = _combine(meta, loc.T, gates.T, x1, ada3, b_off, seq, wts['g_post_ffn'], ybuf, td)
    win = min(WINDOW, k3.shape[1])
    k_new = k3[:, k3.shape[1] - win:].reshape(nb, win, N_KV_HEADS, HEAD_DIM)
    v_new = v3[:, v3.shape[1] - win:].reshape(nb, win, N_KV_HEADS, HEAD_DIM)
    return out.reshape(nb, seq, D_MODEL), k_new, v_new, s_new


def kernel(x_prompt, x_sample, cache_k, cache_v, state_hgrn, c_prompt, c_sample, w_ada, b_ada, g_pre_mix, g_post_mix, g_pre_ffn, g_post_ffn, w_in, attn_sinks, hgrn_lb, g_hgrn, w_br_attn, w_br_hgrn, w_out, w_router, b_router, w_gate_up, b_gate_up, w_down, b_down):
    n_bp = x_prompt.shape[0]
    wr = jnp.pad(w_router[0], ((0, 0), (0, LANES - N_EXPERTS)))
    wr_hi = wr.astype(BF16)
    wr_lo = (wr - wr_hi.astype(F32)).astype(BF16)
    wts = dict(
        g_pre_mix=g_pre_mix[0], g_post_mix=g_post_mix[0], g_pre_ffn=g_pre_ffn[0], g_post_ffn=g_post_ffn[0],
        w_in=w_in[0].astype(BF16), sinks=attn_sinks[0], hgrn_lb=hgrn_lb, g_hgrn=g_hgrn[0],
        w_br_attn=w_br_attn[0].astype(BF16), w_br_hgrn=w_br_hgrn[0].astype(BF16), w_out=w_out[0].astype(BF16),
        w_router_hi=wr_hi, w_router_lo=wr_lo, b_router=b_router[0],
        w_gate_up=w_gate_up[0], b_gate_up=b_gate_up[0], w_down=w_down[0], b_down=b_down[0])
    ada = _ada(jnp.concatenate([c_prompt, c_sample], axis=0), w_ada[0], b_ada[0])
    ada3 = ada.reshape(ada.shape[0], 6, D_MODEL)
    pos_p = jnp.arange(x_prompt.shape[1])
    pos_s = PAST_LEN + jnp.arange(x_sample.shape[1])
    ys, ks, vs, ss = _layer(x_sample, ada3, n_bp, pos_s, cache_k[0], cache_v[0], state_hgrn[0], wts)
    yp, kp, vp, sp = _layer(x_prompt, ada3, 0, pos_p, None, None, None, wts)
    return (yp, ys, kp[None], vp[None], sp[None], ks[None], vs[None], ss[None])
```

```python
import functools

import jax
import jax.numpy as jnp
from jax import lax
from jax.experimental import pallas as pl
from jax.experimental.pallas import tpu as pltpu

F32 = jnp.float32
BF16 = jnp.bfloat16
I32 = jnp.int32

D_MODEL = 1024
PAST_LEN = 1024
CHUNK = 64
HEAD_DIM = 64
N_Q_HEADS = 8
N_KV_HEADS = 2
Q_PER_KV = N_Q_HEADS // N_KV_HEADS
ATT_WIDTH = N_Q_HEADS * HEAD_DIM
KV_WIDTH = N_KV_HEADS * HEAD_DIM
WINDOW = 128
ROPE_THETA = 10000.0
HG_HEADS = 4
HG_DK = 128
HG_DV = 128
HG_WIDTH = HG_HEADS * HG_DK
HG_BLOCK = 16
N_EXPERTS = 32
TOP_K = 4
D_FF = 1024
SWIGLU_LIMIT = 7.0
SWIGLU_ALPHA = 1.702
RMS_EPS = 1e-6
N_IN = ATT_WIDTH + 2 * KV_WIDTH + 4 * HG_WIDTH + 2 * D_MODEL
HG_OFF = ATT_WIDTH + 2 * KV_WIDTH
Z_OFF = HG_OFF + 4 * HG_WIDTH

SORT_CHUNK = 256
ATTN_PHASE_CHUNKS = 8
META_LEN = 3 * N_EXPERTS + 1
LANES = 128
SUBLANES = 8
VMEM_LIMIT = 56 * 1024 * 1024

_NT = (((1,), (1,)), ((), ()))
_TN = (((0,), (0,)), ((), ()))


def _sigmoid(x):
    return 1.0 / (1.0 + jnp.exp(-x))


def _rms(x):
    return x * lax.rsqrt(jnp.mean(x * x, axis=-1, keepdims=True) + RMS_EPS)


def _cparams(n_axes):
    return pltpu.CompilerParams(dimension_semantics=("arbitrary",) * n_axes,
                                vmem_limit_bytes=VMEM_LIMIT)


def _ada_kernel(c_ref, w_ref, b_ref, o_ref):
    c = c_ref[...]
    s = c * _sigmoid(c)
    o_ref[...] = jnp.dot(s, w_ref[...], precision=lax.Precision.HIGHEST,
                         preferred_element_type=F32) + b_ref[...]


def _ada(c, w_ada, b_ada):
    nb = c.shape[0]
    n_out = w_ada.shape[1]
    tn = D_MODEL
    return pl.pallas_call(
        _ada_kernel,
        grid=(n_out // tn,),
        in_specs=[pl.BlockSpec((nb, D_MODEL), lambda j: (0, 0)),
                  pl.BlockSpec((D_MODEL, tn), lambda j: (0, j)),
                  pl.BlockSpec((1, tn), lambda j: (0, j))],
        out_specs=pl.BlockSpec((nb, tn), lambda j: (0, j)),
        out_shape=jax.ShapeDtypeStruct((nb, n_out), F32),
        compiler_params=_cparams(1),
        name="ada",
    )(c, w_ada, b_ada.reshape(1, n_out))


def _proj_kernel(x_ref, mod_ref, g_ref, w_ref, cos_ref, sin_ref,
                 q_ref, k_ref, v_ref, hg_ref, z_ref):
    x = x_ref[...]
    h = _rms(x) * g_ref[...]
    h = h * (1.0 + mod_ref[1:2, :]) + mod_ref[0:1, :]
    hb = h.astype(BF16)
    cos = cos_ref[...]
    sin = sin_ref[...]
    lane = lax.broadcasted_iota(I32, cos.shape, 1)
    first_half = (lane % HEAD_DIM) < (HEAD_DIM // 2)

    def cols(start, width):
        return jnp.dot(hb, w_ref[:, start:start + width], preferred_element_type=F32)

    def rope(p):
        partner = jnp.where(first_half, pltpu.roll(p, LANES - HEAD_DIM // 2, 1),
                            pltpu.roll(p, HEAD_DIM // 2, 1))
        return p * cos + partner * sin

    for j in range(ATT_WIDTH // LANES):
        q = rope(cols(j * LANES, LANES)) * (HEAD_DIM ** -0.5)
        q_ref[:, j * LANES:(j + 1) * LANES] = q.astype(BF16)
    k_ref[...] = rope(cols(ATT_WIDTH, KV_WIDTH))
    v_ref[...] = cols(ATT_WIDTH + KV_WIDTH, KV_WIDTH)
    for j in range(4):
        hg_ref[:, j * HG_WIDTH:(j + 1) * HG_WIDTH] = cols(HG_OFF + j * HG_WIDTH, HG_WIDTH)
    for j in range(4):
        z_ref[:, j * 512:(j + 1) * 512] = cols(Z_OFF + j * 512, 512)


def _proj(x2, ada3, b_off, seq, g_pre, w_in_bf, cos_t, sin_t, tm):
    n = x2.shape[0]
    tpb = seq // tm
    row = lambda i: (i, 0)
    return pl.pallas_call(
        _proj_kernel,
        grid=(n // tm,),
        in_specs=[pl.BlockSpec((tm, D_MODEL), row),
                  pl.BlockSpec((None, 6, D_MODEL), lambda i: (i // tpb + b_off, 0, 0)),
                  pl.BlockSpec((1, D_MODEL), lambda i: (0, 0)),
                  pl.BlockSpec((D_MODEL, N_IN), lambda i: (0, 0)),
                  pl.BlockSpec((tm, LANES), lambda i: (i % tpb, 0)),
                  pl.BlockSpec((tm, LANES), lambda i: (i % tpb, 0))],
        out_specs=[pl.BlockSpec((tm, ATT_WIDTH), row),
                   pl.BlockSpec((tm, KV_WIDTH), row),
                   pl.BlockSpec((tm, KV_WIDTH), row),
                   pl.BlockSpec((tm, 4 * HG_WIDTH), row),
                   pl.BlockSpec((tm, 2 * D_MODEL), row)],
        out_shape=[jax.ShapeDtypeStruct((n, ATT_WIDTH), BF16),
                   jax.ShapeDtypeStruct((n, KV_WIDTH), F32),
                   jax.ShapeDtypeStruct((n, KV_WIDTH), F32),
                   jax.ShapeDtypeStruct((n, 4 * HG_WIDTH), F32),
                   jax.ShapeDtypeStruct((n, 2 * D_MODEL), F32)],
        compiler_params=_cparams(1),
        name="proj",
    )(x2, ada3, g_pre.reshape(1, D_MODEL), w_in_bf, cos_t, sin_t)


def _attn_kernel(sink_ref, q_ref, k_ref, v_ref, o_ref, *, tq, koff):
    t = pl.program_id(1)
    span = WINDOW + CHUNK
    lane = lax.broadcasted_iota(I32, (span, KV_WIDTH), 1)
    top_row = lax.broadcasted_iota(I32, (2 * CHUNK, 1), 0) < CHUNK
    n_chunks = tq // CHUNK
    for c0 in range(0, n_chunks, ATTN_PHASE_CHUNKS):
        units = []
        for c in range(c0, min(c0 + ATTN_PHASE_CHUNKS, n_chunks)):
            r = t * tq + c * CHUNK + koff
            ks = pl.multiple_of(jnp.maximum(r - WINDOW, 0), CHUNK)
            kf = k_ref[pl.ds(ks, span), :]
            vf = v_ref[pl.ds(ks, span), :]
            valid = ks + lax.broadcasted_iota(I32, (1, span), 1) < r + CHUNK
            rows = slice(c * CHUNK, (c + 1) * CHUNK)
            for g in range(N_KV_HEADS):
                own = (lane // HEAD_DIM) == g
                k_own = jnp.where(own, kf, 0.0)
                v_own = jnp.where(own, vf, 0.0)
                k_oth = pltpu.roll(k_own, HEAD_DIM, 1)
                v_oth = pltpu.roll(v_own, HEAD_DIM, 1)
                k_par = (k_own, k_oth) if g == 0 else (k_oth, k_own)
                v_par = (v_own, v_oth) if g == 0 else (v_oth, v_own)
                qq = jnp.concatenate([q_ref[rows, (2 * g + i) * LANES:(2 * g + i + 1) * LANES] for i in range(2)],
                                     axis=0)
                for par in range(2):
                    h_top = Q_PER_KV * g + par
                    units.append((qq, k_par[par].astype(BF16), v_par[par].astype(BF16),
                                  jnp.where(top_row, sink_ref[h_top], sink_ref[h_top + 2]), valid))
        ss = [jnp.where(u[4], lax.dot_general(u[0], u[1], _NT, preferred_element_type=F32), -jnp.inf) for u in units]
        ms = [jnp.maximum(jnp.max(s, axis=-1, keepdims=True), u[3]) for s, u in zip(ss, units)]
        ps = [jnp.exp(s - m) for s, m in zip(ss, ms)]
        ds = [jnp.sum(p, axis=-1, keepdims=True) + jnp.exp(u[3] - m) for p, m, u in zip(ps, ms, units)]
        os_ = [jnp.dot((p / d).astype(BF16), u[2], preferred_element_type=F32) for p, d, u in zip(ps, ds, units)]
        for ci, c in enumerate(range(c0, min(c0 + ATTN_PHASE_CHUNKS, n_chunks))):
            rows = slice(c * CHUNK, (c + 1) * CHUNK)
            for g in range(N_KV_HEADS):
                u0 = (ci * N_KV_HEADS + g) * 2
                acc = os_[u0] + os_[u0 + 1]
                for i in range(2):
                    o_ref[rows, (2 * g + i) * LANES:(2 * g + i + 1) * LANES] = (
                        acc[i * CHUNK:(i + 1) * CHUNK].astype(BF16))


def _attn(q3, k3, v3, sinks, tq, koff):
    nb, tl, _ = q3.shape
    tk = k3.shape[1]
    return pl.pallas_call(
        functools.partial(_attn_kernel, tq=tq, koff=koff),
        grid=(nb, tl // tq),
        in_specs=[pl.BlockSpec(memory_space=pltpu.SMEM),
                  pl.BlockSpec((None, tq, ATT_WIDTH), lambda b, t: (b, t, 0)),
                  pl.BlockSpec((None, tk, KV_WIDTH), lambda b, t: (b, 0, 0)),
                  pl.BlockSpec((None, tk, KV_WIDTH), lambda b, t: (b, 0, 0))],
        out_specs=pl.BlockSpec((None, tq, ATT_WIDTH), lambda b, t: (b, t, 0)),
        out_shape=jax.ShapeDtypeStruct((nb, tl, ATT_WIDTH), BF16),
        compiler_params=_cparams(2),
        name="attn",
    )(sinks, q3, k3, v3)


def _hgrn_kernel(*refs, tc, has_s0):
    if has_s0:
        hg_ref, lbp_ref, gh_ref, s0_ref, y_ref, s_out_ref, st_ref = refs
    else:
        hg_ref, lbp_ref, gh_ref, y_ref, s_out_ref, st_ref = refs
        s0_ref = None
    t = pl.program_id(1)

    @pl.when(t == 0)
    def _():
        for h in range(HG_HEADS):
            st_ref[h] = s0_ref[h].T if has_s0 else jnp.zeros((HG_DV, HG_DK), F32)

    l0 = lbp_ref[0:1, :]
    l1 = lbp_ref[1:2, :]
    lm = jnp.maximum(l0, l1)
    e0 = jnp.exp(l0 - lm)
    lb = e0 / (e0 + jnp.exp(l1 - lm))

    hq = hg_ref[:, 0:HG_WIDTH]
    hf = hg_ref[:, HG_WIDTH:2 * HG_WIDTH]
    hi = hg_ref[:, 2 * HG_WIDTH:3 * HG_WIDTH]
    hz = hg_ref[:, 3 * HG_WIDTH:4 * HG_WIDTH]
    f = lb + (1.0 - lb) * _sigmoid(hf)
    q = hq * _sigmoid(hq)
    kk = 1.0 - f
    g = jnp.log(f)

    r16 = lax.broadcasted_iota(I32, g.shape, 0) % HG_BLOCK
    b = g
    suf = g
    for s in (1, 2, 4, 8):
        b = b + jnp.where(r16 >= s, pltpu.roll(b, s, 0), 0.0)
        suf = suf + jnp.where(r16 < HG_BLOCK - s, pltpu.roll(suf, tc - s, 0), 0.0)
    qt = (q * jnp.exp(b)).astype(BF16)
    kt = (kk * jnp.exp(-b)).astype(BF16)
    kd = (kk * jnp.exp(suf - g)).astype(BF16)
    vb = hi.astype(BF16)
    grp = min(tc, LANES)
    gi = lax.broadcasted_iota(I32, (grp, grp), 0)
    gj = lax.broadcasted_iota(I32, (grp, grp), 1)
    keep = ((gi // HG_BLOCK) == (gj // HG_BLOCK)) & (gi >= gj)
    heads = [slice(h * HG_DK, (h + 1) * HG_DK) for h in range(HG_HEADS)]
    o_groups = [[] for _ in heads]
    for g in range(tc // grp):
        gs = slice(g * grp, (g + 1) * grp)
        a_s = [jnp.where(keep, lax.dot_general(qt[gs, cs], kt[gs, cs], _NT, preferred_element_type=F32), 0.0)
               for cs in heads]
        o_intra = [jnp.dot(a.astype(BF16), vb[gs, cs], preferred_element_type=F32) for a, cs in zip(a_s, heads)]
        o_inter = [[] for _ in heads]
        for j in range(grp // HG_BLOCK):
            r0 = g * grp + j * HG_BLOCK
            rs = slice(r0, r0 + HG_BLOCK)
            sts = [st_ref[h] for h in range(HG_HEADS)]
            ups = [lax.dot_general(vb[rs, cs], kd[rs, cs], _TN, preferred_element_type=F32) for cs in heads]
            for h, cs in enumerate(heads):
                o_inter[h].append(lax.dot_general(qt[rs, cs], sts[h].astype(BF16), _NT, preferred_element_type=F32))
            for h, cs in enumerate(heads):
                dec = jnp.exp(b[r0 + HG_BLOCK - 1:r0 + HG_BLOCK, cs])
                st_ref[h] = sts[h] * dec + ups[h]
        for h in range(HG_HEADS):
            o_groups[h].append(o_intra[h] + jnp.concatenate(o_inter[h], axis=0))
    o_heads = []
    for h, cs in enumerate(heads):
        o_h = jnp.concatenate(o_groups[h], axis=0)
        zs = hz[:, cs]
        o_heads.append(_rms(o_h) * gh_ref[:, cs] * (zs * _sigmoid(zs)))
    y_ref[...] = jnp.concatenate(o_heads, axis=1).astype(BF16)

    @pl.when(t == pl.num_programs(1) - 1)
    def _():
        for h in range(HG_HEADS):
            s_out_ref[h] = st_ref[h].T


def _hgrn(hg3, hgrn_lb, g_hgrn, s0, tc):
    nb, tl, _ = hg3.shape
    has_s0 = s0 is not None
    st_spec = pl.BlockSpec((None, HG_HEADS, HG_DK, HG_DV), lambda b, t: (b, 0, 0, 0))
    in_specs = [pl.BlockSpec((None, tc, 4 * HG_WIDTH), lambda b, t: (b, t, 0)),
                pl.BlockSpec((2, HG_WIDTH), lambda b, t: (0, 0)),
                pl.BlockSpec((1, HG_WIDTH), lambda b, t: (0, 0))]
    args = [hg3, hgrn_lb, g_hgrn.reshape(1, HG_WIDTH)]
    if has_s0:
        in_specs.append(st_spec)
        args.append(s0)
    return pl.pallas_call(
        functools.partial(_hgrn_kernel, tc=tc, has_s0=has_s0),
        grid=(nb, tl // tc),
        in_specs=in_specs,
        out_specs=[pl.BlockSpec((None, tc, HG_WIDTH), lambda b, t: (b, t, 0)), st_spec],
        out_shape=[jax.ShapeDtypeStruct((nb, tl, HG_WIDTH), BF16),
                   jax.ShapeDtypeStruct((nb, HG_HEADS, HG_DK, HG_DV), F32)],
        scratch_shapes=[pltpu.VMEM((HG_HEADS, HG_DV, HG_DK), F32)],
        compiler_params=_cparams(2),
        name="hgrn",
    )(*args)


def _merge_kernel(a_ref, b_ref, z_ref, x_ref, mod_ref, wa_ref, wb_ref, wo_ref, gpost_ref, gpre_ref,
                  wrh_ref, wrl_ref, br_ref, x1_ref, h2_ref, gate_ref, loc_ref, cnt_ref, *, tm, nseg, sub):
    i = pl.program_id(0)
    seg = tm // nseg

    def mod_rows(r, part):
        if nseg == 1:
            return mod_ref[0, r:r + 1, :]
        segs = range(part * sub // seg, (part + 1) * sub // seg)
        return jnp.concatenate([jnp.broadcast_to(mod_ref[s, r:r + 1, :], (seg, D_MODEL)) for s in segs], axis=0)

    @pl.when(i == 0)
    def _():
        cnt_ref[...] = jnp.zeros_like(cnt_ref)

    earlier = (lax.broadcasted_iota(I32, (sub, sub), 0) < lax.broadcasted_iota(I32, (sub, sub), 1)).astype(BF16)
    lower = (lax.broadcasted_iota(I32, (N_EXPERTS, N_EXPERTS), 1)
             < lax.broadcasted_iota(I32, (N_EXPERTS, N_EXPERTS), 0)).astype(BF16)
    eio = lax.broadcasted_iota(I32, (N_EXPERTS, sub), 0).astype(F32)
    tile_lane = lax.broadcasted_iota(I32, cnt_ref.shape, 1)
    parts = range(tm // sub)
    rss = [slice(part * sub, (part + 1) * sub) for part in parts]
    ms = [(_sigmoid(z_ref[rs, 0:D_MODEL]) * jnp.dot(a_ref[rs, :], wa_ref[...], preferred_element_type=F32)
           + _sigmoid(z_ref[rs, D_MODEL:2 * D_MODEL]) * jnp.dot(b_ref[rs, :], wb_ref[...], preferred_element_type=F32))
          for rs in rss]
    ys = [jnp.dot(m.astype(BF16), wo_ref[...], preferred_element_type=F32) for m in ms]
    x1s = [x_ref[rs, :] + mod_rows(2, part) * (_rms(y) * gpost_ref[...]) for part, rs, y in zip(parts, rss, ys)]
    for rs, x1 in zip(rss, x1s):
        x1_ref[rs, :] = x1
    h2s = [(_rms(x1) * gpre_ref[...]) * (1.0 + mod_rows(4, part)) + mod_rows(3, part) for part, x1 in zip(parts, x1s)]
    h_his = [h2.astype(BF16) for h2 in h2s]
    for rs, h_hi in zip(rss, h_his):
        h2_ref[rs, :] = h_hi

    h_los = [(h2 - h_hi.astype(F32)).astype(BF16) for h2, h_hi in zip(h2s, h_his)]
    lgs = [(jnp.dot(h_hi, wrh_ref[...], preferred_element_type=F32)
            + jnp.dot(h_hi, wrl_ref[...], preferred_element_type=F32)
            + jnp.dot(h_lo, wrh_ref[...], preferred_element_type=F32)) for h_hi, h_lo in zip(h_his, h_los)]
    curs = [lg.T[0:N_EXPERTS, :] + br_ref[...] for lg in lgs]
    vals = [[] for _ in parts]
    sels = [[] for _ in parts]
    for k in range(TOP_K):
        for part in parts:
            cur = curs[part]
            mx = jnp.max(cur, axis=0, keepdims=True)
            ik = jnp.min(jnp.where(cur == mx, eio, float(N_EXPERTS)), axis=0, keepdims=True)
            sel = eio == ik
            vals[part].append(mx)
            sels[part].append(sel)
            curs[part] = jnp.where(sel, -jnp.inf, cur)
    for part, rs in zip(parts, rss):
        es = [jnp.exp(v - vals[part][0]) for v in vals[part]]
        den = es[0] + es[1] + es[2] + es[3]
        for k in range(TOP_K):
            gate_ref[k:k + 1, rs] = es[k] / den

    for part, rs in zip(parts, rss):
        onehot = jnp.zeros((N_EXPERTS, sub), F32)
        for k in range(TOP_K):
            onehot = onehot + sels[part][k].astype(F32)
        n8 = jnp.ceil(jnp.sum(onehot, axis=1, keepdims=True) * 0.125) * 8.0
        base = (jnp.dot(lower, jnp.broadcast_to(n8, onehot.shape).astype(BF16), preferred_element_type=F32)
                + jnp.dot(onehot.astype(BF16), earlier, preferred_element_type=F32))
        for k in range(TOP_K):
            loc_ref[k:k + 1, rs] = jnp.sum(jnp.where(sels[part][k], base, 0.0), axis=0, keepdims=True).astype(I32)
        cnt_ref[...] = jnp.where(tile_lane == i * (tm // sub) + part, n8, cnt_ref[...])


def _merge(a2, b2, z2, x2, ada3, b_off, seq, wa, wb, wo, g_post, g_pre, wr_hi, wr_lo, br, tm, sub):
    n = x2.shape[0]
    if tm >= seq:
        nseg = tm // seq
        assert b_off % nseg == 0
        mod_idx = lambda i: (i + b_off // nseg, 0, 0)
    else:
        nseg = 1
        tpb = seq // tm
        mod_idx = lambda i: (i // tpb + b_off, 0, 0)
    row = lambda i: (i, 0)
    const = lambda i: (0, 0)
    col = lambda i: (0, i)
    return pl.pallas_call(
        functools.partial(_merge_kernel, tm=tm, nseg=nseg, sub=sub),
        grid=(n // tm,),
        in_specs=[pl.BlockSpec((tm, ATT_WIDTH), row),
                  pl.BlockSpec((tm, HG_WIDTH), row),
                  pl.BlockSpec((tm, 2 * D_MODEL), row),
                  pl.BlockSpec((tm, D_MODEL), row),
                  pl.BlockSpec((nseg, 6, D_MODEL), mod_idx),
                  pl.BlockSpec((ATT_WIDTH, D_MODEL), const),
                  pl.BlockSpec((HG_WIDTH, D_MODEL), const),
                  pl.BlockSpec((D_MODEL, D_MODEL), const),
                  pl.BlockSpec((1, D_MODEL), const),
                  pl.BlockSpec((1, D_MODEL), const),
                  pl.BlockSpec((D_MODEL, LANES), const),
                  pl.BlockSpec((D_MODEL, LANES), const),
                  pl.BlockSpec((N_EXPERTS, 1), const)],
        out_specs=[pl.BlockSpec((tm, D_MODEL), row),
                   pl.BlockSpec((tm, D_MODEL), row),
                   pl.BlockSpec((TOP_K, tm), col),
                   pl.BlockSpec((TOP_K, tm), col),
                   pl.BlockSpec((N_EXPERTS, n // sub), const)],
        out_shape=[jax.ShapeDtypeStruct((n, D_MODEL), F32),
                   jax.ShapeDtypeStruct((n, D_MODEL), BF16),
                   jax.ShapeDtypeStruct((TOP_K, n), F32),
                   jax.ShapeDtypeStruct((TOP_K, n), I32),
                   jax.ShapeDtypeStruct((N_EXPERTS, n // sub), F32)],
        compiler_params=_cparams(1),
        name="merge",
    )(a2, b2, z2, x2, ada3, wa, wb, wo, g_post.reshape(1, D_MODEL), g_pre.reshape(1, D_MODEL),
      wr_hi, wr_lo, br.reshape(N_EXPERTS, 1))


def _pow2_pieces(n, max_piece, fn):
    sz = max_piece
    while sz >= SUBLANES:
        off = n & ~(2 * sz - 1)

        @pl.when((n & sz) != 0)
        def _(off=off, sz=sz):
            fn(off, sz)

        sz //= 2


def _start_groups(meta_ref, td, make_copy):
    def body(e, _):
        loc0 = meta_ref[0, e]
        n8 = meta_ref[0, N_EXPERTS + e]
        glob0 = meta_ref[0, 2 * N_EXPERTS + e]
        _pow2_pieces(n8, td, lambda off, sz: make_copy(pl.multiple_of(loc0 + off, SUBLANES),
                                                       pl.multiple_of(glob0 + off, SUBLANES), sz).start())
        return 0

    lax.fori_loop(0, N_EXPERTS, body, 0, unroll=2)


def _wait_groups(meta_ref, n_sorted, make_wait):
    max_piece = 1 << (n_sorted.bit_length() - 1)
    _pow2_pieces(meta_ref[0, 3 * N_EXPERTS], max_piece, lambda off, sz: make_wait(sz).wait())


def _dispatch_kernel(meta_ref, mprev_ref, tail_ref, loc_a_ref, h_a_ref, loc_b_ref, h_b_ref, buf_ref,
                     srt_ref, zero_ref, sem, zsem, *, td, blk, n_a):
    i = pl.program_id(0)
    slot = i % 2
    n_sorted = srt_ref.shape[1]

    @pl.when(i == 0)
    def _():
        zero_ref[...] = jnp.zeros_like(zero_ref)

        def zero_copy(start, off, sz):
            return pltpu.make_async_copy(zero_ref.at[pl.ds(0, sz)],
                                         buf_ref.at[pl.ds(pl.multiple_of(start + off, SUBLANES), sz)], zsem)

        for act in (lambda c: c.start(), lambda c: c.wait()):
            def body(e, _, act=act):
                start = tail_ref[0, e]
                _pow2_pieces(tail_ref[1, e], blk // 2, lambda off, sz: act(zero_copy(start, off, sz)))
                return 0

            lax.fori_loop(0, N_EXPERTS, body, 0)

            def unused(j, _, act=act):
                act(zero_copy(tail_ref[0, N_EXPERTS], j * (blk // 2), blk // 2))
                return 0

            lax.fori_loop(0, tail_ref[1, N_EXPERTS] // (blk // 2), unused, 0)

    def sort_rows(loc_ref, h_ref):
        for j0 in range(0, n_sorted, SORT_CHUNK):
            rio = j0 + lax.broadcasted_iota(I32, (SORT_CHUNK, td), 0)
            hit = rio == loc_ref[0:1, :]
            for k in range(1, TOP_K):
                hit = hit | (rio == loc_ref[k:k + 1, :])
            srt_ref[slot, j0:j0 + SORT_CHUNK, :] = jnp.dot(hit.astype(BF16), h_ref[...],
                                                           preferred_element_type=F32)

    @pl.when(i < n_a)
    def _():
        sort_rows(loc_a_ref, h_a_ref)

    @pl.when(i >= n_a)
    def _():
        sort_rows(loc_b_ref, h_b_ref)

    def make_copy(s):
        return lambda loc0, glob0, sz: pltpu.make_async_copy(
            srt_ref.at[s, pl.ds(loc0, sz)], buf_ref.at[pl.ds(glob0, sz)], sem.at[s])

    _start_groups(meta_ref, td, make_copy(slot))

    @pl.when(i > 0)
    def _():
        _wait_groups(mprev_ref, n_sorted, lambda sz: make_copy(1 - slot)(0, 0, sz))

    @pl.when(i == pl.num_programs(0) - 1)
    def _():
        _wait_groups(meta_ref, n_sorted, lambda sz: make_copy(slot)(0, 0, sz))


def _dispatch(h2_a, loc_a, h2_b, loc_b, meta, tail, n_rows, td, blk):
    n_a, n_b = h2_a.shape[0] // td, h2_b.shape[0] // td
    n_sorted = TOP_K * td + SUBLANES * N_EXPERTS
    tile_a = lambda i: jnp.minimum(i, n_a - 1)
    tile_b = lambda i: jnp.maximum(i - n_a, 0)
    return pl.pallas_call(
        functools.partial(_dispatch_kernel, td=td, blk=blk, n_a=n_a),
        grid=(n_a + n_b,),
        in_specs=[pl.BlockSpec((None, 1, META_LEN), lambda i: (i, 0, 0), memory_space=pltpu.SMEM),
                  pl.BlockSpec((None, 1, META_LEN), lambda i: (jnp.maximum(i - 1, 0), 0, 0),
                               memory_space=pltpu.SMEM),
                  pl.BlockSpec(memory_space=pltpu.SMEM),
                  pl.BlockSpec((TOP_K, td), lambda i: (0, tile_a(i))),
                  pl.BlockSpec((td, D_MODEL), lambda i: (tile_a(i), 0)),
                  pl.BlockSpec((TOP_K, td), lambda i: (0, tile_b(i))),
                  pl.BlockSpec((td, D_MODEL), lambda i: (tile_b(i), 0))],
        out_specs=pl.BlockSpec(memory_space=pl.ANY),
        out_shape=jax.ShapeDtypeStruct((n_rows, D_MODEL), F32),
        scratch_shapes=[pltpu.VMEM((2, n_sorted, D_MODEL), F32), pltpu.VMEM((blk // 2, D_MODEL), F32),
                        pltpu.SemaphoreType.DMA((2,)), pltpu.SemaphoreType.DMA],
        compiler_params=_cparams(1),
        name="dispatch",
    )(meta, meta, tail, loc_a, h2_a, loc_b, h2_b)


def _expert_kernel(be_ref, nu_ref, x_ref, wgu_ref, bgu_ref, wd_ref, bd_ref, y_ref, wgu_bf, wd_bf):
    i = pl.program_id(0)

    @pl.when(i < nu_ref[0])
    def _():
        @pl.when((i == 0) | (be_ref[i] != be_ref[jnp.maximum(i - 1, 0)]))
        def _():
            wgu_bf[...] = wgu_ref[...].astype(BF16)
            wd_bf[...] = wd_ref[...].astype(BF16)

        gu = jnp.dot(x_ref[...].astype(BF16), wgu_bf[...], preferred_element_type=F32) + bgu_ref[...]
        gate = jnp.minimum(gu[:, :D_FF], SWIGLU_LIMIT)
        up = jnp.clip(gu[:, D_FF:], -SWIGLU_LIMIT, SWIGLU_LIMIT)
        act = gate * _sigmoid(SWIGLU_ALPHA * gate) * (up + 1.0)
        y_ref[...] = jnp.dot(act.astype(BF16), wd_bf[...], preferred_element_type=F32) + bd_ref[...]

    @pl.when(i >= nu_ref[0])
    def _():
        y_ref[...] = jnp.zeros_like(y_ref)


def _experts(buf, blk_expert, n_used, wgu, bgu, wd, bd, blk):
    n_rows = buf.shape[0]
    used = lambda i, be, nu: (jnp.minimum(i, nu[0] - 1), 0)
    grid_spec = pltpu.PrefetchScalarGridSpec(
        num_scalar_prefetch=2,
        grid=(n_rows // blk,),
        in_specs=[pl.BlockSpec((blk, D_MODEL), used),
                  pl.BlockSpec((None, D_MODEL, 2 * D_FF), lambda i, be, nu: (be[i], 0, 0)),
                  pl.BlockSpec((None, 1, 2 * D_FF), lambda i, be, nu: (be[i], 0, 0)),
                  pl.BlockSpec((None, D_FF, D_MODEL), lambda i, be, nu: (be[i], 0, 0)),
                  pl.BlockSpec((None, 1, D_MODEL), lambda i, be, nu: (be[i], 0, 0))],
        out_specs=pl.BlockSpec((blk, D_MODEL), lambda i, be, nu: (i, 0)),
        scratch_shapes=[pltpu.VMEM((D_MODEL, 2 * D_FF), BF16), pltpu.VMEM((D_FF, D_MODEL), BF16)],
    )
    return pl.pallas_call(
        _expert_kernel,
        grid_spec=grid_spec,
        out_shape=jax.ShapeDtypeStruct((n_rows, D_MODEL), F32),
        compiler_params=_cparams(1),
        name="experts",
    )(blk_expert, n_used, buf, wgu, bgu.reshape(N_EXPERTS, 1, 2 * D_FF), wd, bd.reshape(N_EXPERTS, 1, D_MODEL))


def _combine_kernel(meta_ref, mnext_ref, loc_ref, gate_ref, x1_ref, mod_ref, gpost_ref, ybuf_ref, o_ref,
                    ys_ref, sem, *, td, nseg):
    i = pl.program_id(0)
    slot = i % 2
    n_sorted = ys_ref.shape[1]

    def make_copy(s):
        return lambda loc0, glob0, sz: pltpu.make_async_copy(
            ybuf_ref.at[pl.ds(glob0, sz)], ys_ref.at[s, pl.ds(loc0, sz)], sem.at[s])

    @pl.when(i == 0)
    def _():
        ys_ref[...] = jnp.zeros_like(ys_ref)
        _start_groups(meta_ref, td, make_copy(0))

    @pl.when(i + 1 < pl.num_programs(0))
    def _():
        _start_groups(mnext_ref, td, make_copy(1 - slot))

    _wait_groups(meta_ref, n_sorted, lambda sz: make_copy(slot)(0, 0, sz))
    mo = None
    for j0 in range(0, n_sorted, SORT_CHUNK):
        jio = j0 + lax.broadcasted_iota(I32, (td, SORT_CHUNK), 1)
        w = jnp.zeros((td, SORT_CHUNK), F32)
        for k in range(TOP_K):
            w = w + jnp.where(jio == loc_ref[:, k:k + 1], gate_ref[:, k:k + 1], 0.0)
        w_hi = w.astype(BF16)
        w_lo = (w - w_hi.astype(F32)).astype(BF16)
        y = ys_ref[slot, j0:j0 + SORT_CHUNK, :].astype(BF16)
        part = jnp.dot(w_hi, y, preferred_element_type=F32) + jnp.dot(w_lo, y, preferred_element_type=F32)
        mo = part if mo is None else mo + part
    if nseg == 1:
        gate2 = mod_ref[0, 5:6, :]
    else:
        gate2 = jnp.concatenate([jnp.broadcast_to(mod_ref[s, 5:6, :], (td // nseg, D_MODEL)) for s in range(nseg)],
                                axis=0)
    o_ref[...] = x1_ref[...] + gate2 * (_rms(mo) * gpost_ref[...])


def _combine(meta, loc_t, gates_t, x1, ada3, b_off, seq, g_post, ybuf, td):
    n = x1.shape[0]
    n_tiles = n // td
    n_sorted = TOP_K * td + SUBLANES * N_EXPERTS
    if td >= seq:
        nseg = td // seq
        assert b_off % nseg == 0
        mod_idx = lambda i: (i + b_off // nseg, 0, 0)
    else:
        nseg = 1
        tpb = seq // td
        mod_idx = lambda i: (i // tpb + b_off, 0, 0)
    return pl.pallas_call(
        functools.partial(_combine_kernel, td=td, nseg=nseg),
        grid=(n_tiles,),
        in_specs=[pl.BlockSpec((None, 1, META_LEN), lambda i: (i, 0, 0), memory_space=pltpu.SMEM),
                  pl.BlockSpec((None, 1, META_LEN), lambda i: (jnp.minimum(i + 1, n_tiles - 1), 0, 0),
                               memory_space=pltpu.SMEM),
                  pl.BlockSpec((td, TOP_K), lambda i: (i, 0)),
                  pl.BlockSpec((td, TOP_K), lambda i: (i, 0)),
                  pl.BlockSpec((td, D_MODEL), lambda i: (i, 0)),
                  pl.BlockSpec((nseg, 6, D_MODEL), mod_idx),
                  pl.BlockSpec((1, D_MODEL), lambda i: (0, 0)),
                  pl.BlockSpec(memory_space=pl.ANY)],
        out_specs=pl.BlockSpec((td, D_MODEL), lambda i: (i, 0)),
        out_shape=jax.ShapeDtypeStruct((n, D_MODEL), F32),
        scratch_shapes=[pltpu.VMEM((2, n_sorted, D_MODEL), F32), pltpu.SemaphoreType.DMA((2,))],
        compiler_params=_cparams(1),
        name="combine",
    )(meta, meta, loc_t, gates_t, x1, ada3, g_post.reshape(1, D_MODEL), ybuf)


def _tile_sizes(nb, seq):
    n = nb * seq
    big = n >= 8192
    tmm = 512 if seq >= 512 else n
    return dict(
        tm=min(seq, 512),
        tmm=tmm,
        tq=min(seq, 512),
        tc=min(seq, 512),
        td=min(tmm, 256),
        blk=512 if big else 128,
    )


def _rope_tables(pos):
    half = HEAD_DIM // 2
    inv_freq = ROPE_THETA ** (-jnp.arange(half, dtype=F32) / half)
    ang = pos.astype(F32)[:, None] * inv_freq[None, :]
    cos, sin = jnp.cos(ang), jnp.sin(ang)
    reps = LANES // HEAD_DIM
    return (jnp.tile(jnp.concatenate([cos, cos], axis=1), (1, reps)),
            jnp.tile(jnp.concatenate([-sin, sin], axis=1), (1, reps)))


def _route_plan(cnt8, n_assign, blk):
    n_tiles = cnt8.shape[1]
    n_blocks = -(-(n_assign + n_tiles * N_EXPERTS * (SUBLANES - 1) + N_EXPERTS * (blk - 1)) // blk)
    tot = jnp.sum(cnt8, axis=1)
    padded = (tot + blk - 1) // blk * blk
    pend = jnp.cumsum(padded)
    pstart = pend - padded
    glob0 = pstart[:, None] + jnp.cumsum(cnt8, axis=1) - cnt8
    loc0 = jnp.cumsum(cnt8, axis=0) - cnt8
    meta = jnp.concatenate([loc0.T, cnt8.T, glob0.T, jnp.sum(cnt8, axis=0)[:, None]],
                           axis=1).reshape(n_tiles, 1, META_LEN)
    blk_row = jnp.arange(n_blocks, dtype=I32) * blk
    blk_expert = jnp.minimum(jnp.sum((pend[None, :] <= blk_row[:, None]).astype(I32), axis=1), N_EXPERTS - 1)
    n_used = (pend[-1:] // blk).astype(I32)
    n_rows = n_blocks * blk
    tail = jnp.stack([jnp.concatenate([pstart + tot, pend[-1:]]),
                      jnp.concatenate([padded - tot, n_rows - pend[-1:]])])
    return meta.astype(I32), tail.astype(I32), blk_expert, n_used, n_rows


def _mixer_and_route(x, ada3, b_off, pos, k_past, v_past, s0, wts):
    nb, seq, _ = x.shape
    n = nb * seq
    ts = _tile_sizes(nb, seq)
    x2 = x.reshape(n, D_MODEL)
    cos_t, sin_t = _rope_tables(pos)
    q, k, v, hg, z = _proj(x2, ada3, b_off, seq, wts['g_pre_mix'], wts['w_in'], cos_t, sin_t, ts['tm'])
    k3 = k.reshape(nb, seq, KV_WIDTH)
    v3 = v.reshape(nb, seq, KV_WIDTH)
    if k_past is None:
        koff = 0
    else:
        rows = k_past.shape[1]
        koff = rows
        k3 = jnp.concatenate([k_past.reshape(nb, rows, KV_WIDTH), k3], axis=1)
        v3 = jnp.concatenate([v_past.reshape(nb, rows, KV_WIDTH), v3], axis=1)
    y_att = _attn(q.reshape(nb, seq, ATT_WIDTH), k3, v3, wts['sinks'], ts['tq'], koff)
    y_hg, s_new = _hgrn(hg.reshape(nb, seq, 4 * HG_WIDTH), wts['hgrn_lb'], wts['g_hgrn'], s0, ts['tc'])
    x1, h2, gates, loc, cnt = _merge(
        y_att.reshape(n, ATT_WIDTH), y_hg.reshape(n, HG_WIDTH), z, x2, ada3, b_off, seq,
        wts['w_br_attn'], wts['w_br_hgrn'], wts['w_out'], wts['g_post_mix'], wts['g_pre_ffn'],
        wts['w_router_hi'], wts['w_router_lo'], wts['b_router'], ts['tmm'], ts['td'])
    win = min(WINDOW, k3.shape[1])
    k_new = k3[:, k3.shape[1] - win:].reshape(nb, win, N_KV_HEADS, HEAD_DIM)
    v_new = v3[:, v3.shape[1] - win:].reshape(nb, win, N_KV_HEADS, HEAD_DIM)
    route = dict(x1=x1, h2=h2, gates=gates, loc=loc, cnt=cnt.astype(I32), td=ts['td'], blk=ts['blk'],
                 b_off=b_off, shape=(nb, seq))
    return route, k_new, v_new, s_new


def _moe(routes, ada3, wts):
    ra, rb = routes
    assert ra['td'] == rb['td']
    blk = max(r['blk'] for r in routes)
    n_assign = sum(r['h2'].shape[0] for r in routes) * TOP_K
    meta, tail, blk_expert, n_used, n_rows = _route_plan(
        jnp.concatenate([r['cnt'] for r in routes], axis=1), n_assign, blk)
    metas, t0 = [], 0
    for r in routes:
        metas.append(meta[t0:t0 + r['cnt'].shape[1]])
        t0 += r['cnt'].shape[1]
    buf = _dispatch(ra['h2'], ra['loc'], rb['h2'], rb['loc'], meta, tail, n_rows, ra['td'], blk)
    ybuf = _experts(buf, blk_expert, n_used, wts['w_gate_up'], wts['b_gate_up'], wts['w_down'], wts['b_down'], blk)
    outs = []
    for r, m in zip(routes, metas):
        nb, seq = r['shape']
        out = _combine(m, r['loc'].T, r['gates'].T, r['x1'], ada3, r['b_off'], seq, wts['g_post_ffn'], ybuf, r['td'])
        outs.append(out.reshape(nb, seq, D_MODEL))
    return outs


def kernel(x_prompt, x_sample, cache_k, cache_v, state_hgrn, c_prompt, c_sample, w_ada, b_ada, g_pre_mix, g_post_mix, g_pre_ffn, g_post_ffn, w_in, attn_sinks, hgrn_lb, g_hgrn, w_br_attn, w_br_hgrn, w_out, w_router, b_router, w_gate_up, b_gate_up, w_down, b_down):
    n_bp = x_prompt.shape[0]
    wr = jnp.pad(w_router[0], ((0, 0), (0, LANES - N_EXPERTS)))
    wr_hi = wr.astype(BF16)
    wr_lo = (wr - wr_hi.astype(F32)).astype(BF16)
    wts = dict(
        g_pre_mix=g_pre_mix[0], g_post_mix=g_post_mix[0], g_pre_ffn=g_pre_ffn[0], g_post_ffn=g_post_ffn[0],
        w_in=w_in[0].astype(BF16), sinks=attn_sinks[0], hgrn_lb=hgrn_lb, g_hgrn=g_hgrn[0],
        w_br_attn=w_br_attn[0].astype(BF16), w_br_hgrn=w_br_hgrn[0].astype(BF16), w_out=w_out[0].astype(BF16),
        w_router_hi=wr_hi, w_router_lo=wr_lo, b_router=b_router[0],
        w_gate_up=w_gate_up[0], b_gate_up=b_gate_up[0], w_down=w_down[0], b_down=b_down[0])
    ada = _ada(jnp.concatenate([c_prompt, c_sample], axis=0), w_ada[0], b_ada[0])
    ada3 = ada.reshape(ada.shape[0], 6, D_MODEL)
    pos_p = jnp.arange(x_prompt.shape[1])
    pos_s = PAST_LEN + jnp.arange(x_sample.shape[1])
    route_s, ks, vs, ss = _mixer_and_route(x_sample, ada3, n_bp, pos_s, cache_k[0], cache_v[0], state_hgrn[0], wts)
    route_p, kp, vp, sp = _mixer_and_route(x_prompt, ada3, 0, pos_p, None, None, None, wts)
    ys, yp = _moe([route_s, route_p], ada3, wts)
    return (yp, ys, kp[None], vp[None], sp[None], ks[None], vs[None], ss[None])
```

```python
import functools

import jax
import jax.numpy as jnp
from jax import lax
from jax.experimental import pallas as pl
from jax.experimental.pallas import tpu as pltpu

F32 = jnp.float32
BF16 = jnp.bfloat16
I32 = jnp.int32

D_MODEL = 1024
PAST_LEN = 1024
CHUNK = 64
HEAD_DIM = 64
N_Q_HEADS = 8
N_KV_HEADS = 2
Q_PER_KV = N_Q_HEADS // N_KV_HEADS
ATT_WIDTH = N_Q_HEADS * HEAD_DIM
KV_WIDTH = N_KV_HEADS * HEAD_DIM
WINDOW = 128
ROPE_THETA = 10000.0
HG_HEADS = 4
HG_DK = 128
HG_DV = 128
HG_WIDTH = HG_HEADS * HG_DK
HG_BLOCK = 16
N_EXPERTS = 32
TOP_K = 4
D_FF = 1024
SWIGLU_LIMIT = 7.0
SWIGLU_ALPHA = 1.702
RMS_EPS = 1e-6
N_IN = ATT_WIDTH + 2 * KV_WIDTH + 4 * HG_WIDTH + 2 * D_MODEL
HG_OFF = ATT_WIDTH + 2 * KV_WIDTH
Z_OFF = HG_OFF + 4 * HG_WIDTH

SORT_CHUNK = 256
ATTN_PHASE_CHUNKS = 8
META_LEN = 3 * N_EXPERTS + 1
LANES = 128
SUBLANES = 8
VMEM_LIMIT = 56 * 1024 * 1024

_NT = (((1,), (1,)), ((), ()))
_TN = (((0,), (0,)), ((), ()))


def _sigmoid(x):
    return 1.0 / (1.0 + jnp.exp(-x))


def _rms(x):
    return x * lax.rsqrt(jnp.mean(x * x, axis=-1, keepdims=True) + RMS_EPS)


def _cparams(n_axes):
    return pltpu.CompilerParams(dimension_semantics=("arbitrary",) * n_axes,
                                vmem_limit_bytes=VMEM_LIMIT)


def _ada_kernel(c_ref, w_ref, b_ref, o_ref):
    c = c_ref[...]
    s = c * _sigmoid(c)
    o_ref[...] = jnp.dot(s, w_ref[...], precision=lax.Precision.HIGHEST,
                         preferred_element_type=F32) + b_ref[...]


def _ada(c, w_ada, b_ada):
    nb = c.shape[0]
    n_out = w_ada.shape[1]
    tn = D_MODEL
    return pl.pallas_call(
        _ada_kernel,
        grid=(n_out // tn,),
        in_specs=[pl.BlockSpec((nb, D_MODEL), lambda j: (0, 0)),
                  pl.BlockSpec((D_MODEL, tn), lambda j: (0, j)),
                  pl.BlockSpec((1, tn), lambda j: (0, j))],
        out_specs=pl.BlockSpec((nb, tn), lambda j: (0, j)),
        out_shape=jax.ShapeDtypeStruct((nb, n_out), F32),
        compiler_params=_cparams(1),
        name="ada",
    )(c, w_ada, b_ada.reshape(1, n_out))


def _proj_kernel(x_ref, mod_ref, g_ref, w_ref, cos_ref, sin_ref,
                 q_ref, k_ref, v_ref, hg_ref, z_ref):
    tm = x_ref.shape[0]
    n_sub = 2 if tm >= 512 else 1
    rows = tm // n_sub

    def normed(s):
        x = x_ref[s * rows:(s + 1) * rows, :]
        h = _rms(x) * g_ref[...]
        h = h * (1.0 + mod_ref[1:2, :]) + mod_ref[0:1, :]
        return h.astype(BF16)

    def wide(s, hb):
        rs = slice(s * rows, (s + 1) * rows)
        for j in range(4):
            hg_ref[rs, j * HG_WIDTH:(j + 1) * HG_WIDTH] = jnp.dot(
                hb, w_ref[:, HG_OFF + j * HG_WIDTH:HG_OFF + (j + 1) * HG_WIDTH], preferred_element_type=F32)
        for j in range(4):
            z_ref[rs, j * 512:(j + 1) * 512] = jnp.dot(
                hb, w_ref[:, Z_OFF + j * 512:Z_OFF + (j + 1) * 512], preferred_element_type=F32)

    def heads(s, hb):
        rs = slice(s * rows, (s + 1) * rows)
        cos = cos_ref[rs, :]
        sin = sin_ref[rs, :]
        lane = lax.broadcasted_iota(I32, cos.shape, 1)
        first_half = (lane % HEAD_DIM) < (HEAD_DIM // 2)

        def cols(start, width):
            return jnp.dot(hb, w_ref[:, start:start + width], preferred_element_type=F32)

        def rope(p):
            partner = jnp.where(first_half, pltpu.roll(p, LANES - HEAD_DIM // 2, 1),
                                pltpu.roll(p, HEAD_DIM // 2, 1))
            return p * cos + partner * sin

        for j in range(ATT_WIDTH // LANES):
            q = rope(cols(j * LANES, LANES)) * (HEAD_DIM ** -0.5)
            q_ref[rs, j * LANES:(j + 1) * LANES] = q.astype(BF16)
        k_ref[rs, :] = rope(cols(ATT_WIDTH, KV_WIDTH))
        v_ref[rs, :] = cols(ATT_WIDTH + KV_WIDTH, KV_WIDTH)

    hb = normed(0)
    for s in range(n_sub):
        wide(s, hb)
        hb_next = normed(s + 1) if s + 1 < n_sub else None
        heads(s, hb)
        hb = hb_next


def _proj(x2, ada3, b_off, seq, g_pre, w_in_bf, cos_t, sin_t, tm):
    n = x2.shape[0]
    tpb = seq // tm
    row = lambda i: (i, 0)
    return pl.pallas_call(
        _proj_kernel,
        grid=(n // tm,),
        in_specs=[pl.BlockSpec((tm, D_MODEL), row),
                  pl.BlockSpec((None, 6, D_MODEL), lambda i: (i // tpb + b_off, 0, 0)),
                  pl.BlockSpec((1, D_MODEL), lambda i: (0, 0)),
                  pl.BlockSpec((D_MODEL, N_IN), lambda i: (0, 0)),
                  pl.BlockSpec((tm, LANES), lambda i: (i % tpb, 0)),
                  pl.BlockSpec((tm, LANES), lambda i: (i % tpb, 0))],
        out_specs=[pl.BlockSpec((tm, ATT_WIDTH), row),
                   pl.BlockSpec((tm, KV_WIDTH), row),
                   pl.BlockSpec((tm, KV_WIDTH), row),
                   pl.BlockSpec((tm, 4 * HG_WIDTH), row),
                   pl.BlockSpec((tm, 2 * D_MODEL), row)],
        out_shape=[jax.ShapeDtypeStruct((n, ATT_WIDTH), BF16),
                   jax.ShapeDtypeStruct((n, KV_WIDTH), F32),
                   jax.ShapeDtypeStruct((n, KV_WIDTH), F32),
                   jax.ShapeDtypeStruct((n, 4 * HG_WIDTH), F32),
                   jax.ShapeDtypeStruct((n, 2 * D_MODEL), F32)],
        compiler_params=_cparams(1),
        name="proj",
    )(x2, ada3, g_pre.reshape(1, D_MODEL), w_in_bf, cos_t, sin_t)


def _attn_kernel(sink_ref, q_ref, k_ref, v_ref, o_ref, *, tq, koff):
    t = pl.program_id(1)
    span = WINDOW + CHUNK
    lane = lax.broadcasted_iota(I32, (span, KV_WIDTH), 1)
    top_row = lax.broadcasted_iota(I32, (2 * CHUNK, 1), 0) < CHUNK
    n_chunks = tq // CHUNK
    for c0 in range(0, n_chunks, ATTN_PHASE_CHUNKS):
        units = []
        for c in range(c0, min(c0 + ATTN_PHASE_CHUNKS, n_chunks)):
            r = t * tq + c * CHUNK + koff
            ks = pl.multiple_of(jnp.maximum(r - WINDOW, 0), CHUNK)
            kf = k_ref[pl.ds(ks, span), :]
            vf = v_ref[pl.ds(ks, span), :]
            valid = ks + lax.broadcasted_iota(I32, (1, span), 1) < r + CHUNK
            rows = slice(c * CHUNK, (c + 1) * CHUNK)
            for g in range(N_KV_HEADS):
                own = (lane // HEAD_DIM) == g
                k_own = jnp.where(own, kf, 0.0)
                v_own = jnp.where(own, vf, 0.0)
                k_oth = pltpu.roll(k_own, HEAD_DIM, 1)
                v_oth = pltpu.roll(v_own, HEAD_DIM, 1)
                k_par = (k_own, k_oth) if g == 0 else (k_oth, k_own)
                v_par = (v_own, v_oth) if g == 0 else (v_oth, v_own)
                qq = jnp.concatenate([q_ref[rows, (2 * g + i) * LANES:(2 * g + i + 1) * LANES] for i in range(2)],
                                     axis=0)
                for par in range(2):
                    h_top = Q_PER_KV * g + par
                    units.append((qq, k_par[par].astype(BF16), v_par[par].astype(BF16),
                                  jnp.where(top_row, sink_ref[h_top], sink_ref[h_top + 2]), valid))
        ss = [jnp.where(u[4], lax.dot_general(u[0], u[1], _NT, preferred_element_type=F32), -jnp.inf) for u in units]
        ms = [jnp.maximum(jnp.max(s, axis=-1, keepdims=True), u[3]) for s, u in zip(ss, units)]
        ps = [jnp.exp(s - m) for s, m in zip(ss, ms)]
        ds = [jnp.sum(p, axis=-1, keepdims=True) + jnp.exp(u[3] - m) for p, m, u in zip(ps, ms, units)]
        os_ = [jnp.dot((p / d).astype(BF16), u[2], preferred_element_type=F32) for p, d, u in zip(ps, ds, units)]
        for ci, c in enumerate(range(c0, min(c0 + ATTN_PHASE_CHUNKS, n_chunks))):
            rows = slice(c * CHUNK, (c + 1) * CHUNK)
            for g in range(N_KV_HEADS):
                u0 = (ci * N_KV_HEADS + g) * 2
                acc = os_[u0] + os_[u0 + 1]
                for i in range(2):
                    o_ref[rows, (2 * g + i) * LANES:(2 * g + i + 1) * LANES] = (
                        acc[i * CHUNK:(i + 1) * CHUNK].astype(BF16))


def _attn(q3, k3, v3, sinks, tq, koff):
    nb, tl, _ = q3.shape
    tk = k3.shape[1]
    return pl.pallas_call(
        functools.partial(_attn_kernel, tq=tq, koff=koff),
        grid=(nb, tl // tq),
        in_specs=[pl.BlockSpec(memory_space=pltpu.SMEM),
                  pl.BlockSpec((None, tq, ATT_WIDTH), lambda b, t: (b, t, 0)),
                  pl.BlockSpec((None, tk, KV_WIDTH), lambda b, t: (b, 0, 0)),
                  pl.BlockSpec((None, tk, KV_WIDTH), lambda b, t: (b, 0, 0))],
        out_specs=pl.BlockSpec((None, tq, ATT_WIDTH), lambda b, t: (b, t, 0)),
        out_shape=jax.ShapeDtypeStruct((nb, tl, ATT_WIDTH), BF16),
        compiler_params=_cparams(2),
        name="attn",
    )(sinks, q3, k3, v3)


def _hgrn_kernel(*refs, tc, has_s0):
    if has_s0:
        hg_ref, lbp_ref, gh_ref, s0_ref, y_ref, s_out_ref, st_ref = refs
    else:
        hg_ref, lbp_ref, gh_ref, y_ref, s_out_ref, st_ref = refs
        s0_ref = None
    t = pl.program_id(1)

    @pl.when(t == 0)
    def _():
        for h in range(HG_HEADS):
            st_ref[h] = s0_ref[h].T if has_s0 else jnp.zeros((HG_DV, HG_DK), F32)

    l0 = lbp_ref[0:1, :]
    l1 = lbp_ref[1:2, :]
    lm = jnp.maximum(l0, l1)
    e0 = jnp.exp(l0 - lm)
    lb = e0 / (e0 + jnp.exp(l1 - lm))

    hq = hg_ref[:, 0:HG_WIDTH]
    hf = hg_ref[:, HG_WIDTH:2 * HG_WIDTH]
    hi = hg_ref[:, 2 * HG_WIDTH:3 * HG_WIDTH]
    hz = hg_ref[:, 3 * HG_WIDTH:4 * HG_WIDTH]
    f = lb + (1.0 - lb) * _sigmoid(hf)
    q = hq * _sigmoid(hq)
    kk = 1.0 - f
    g = jnp.log(f)

    r16 = lax.broadcasted_iota(I32, g.shape, 0) % HG_BLOCK
    b = g
    suf = g
    for s in (1, 2, 4, 8):
        b = b + jnp.where(r16 >= s, pltpu.roll(b, s, 0), 0.0)
        suf = suf + jnp.where(r16 < HG_BLOCK - s, pltpu.roll(suf, tc - s, 0), 0.0)
    qt = (q * jnp.exp(b)).astype(BF16)
    kt = (kk * jnp.exp(-b)).astype(BF16)
    kd = (kk * jnp.exp(suf - g)).astype(BF16)
    vb = hi.astype(BF16)
    grp = min(tc, LANES)
    gi = lax.broadcasted_iota(I32, (grp, grp), 0)
    gj = lax.broadcasted_iota(I32, (grp, grp), 1)
    keep = ((gi // HG_BLOCK) == (gj // HG_BLOCK)) & (gi >= gj)
    heads = [slice(h * HG_DK, (h + 1) * HG_DK) for h in range(HG_HEADS)]
    o_groups = [[] for _ in heads]
    for g in range(tc // grp):
        gs = slice(g * grp, (g + 1) * grp)
        a_s = [jnp.where(keep, lax.dot_general(qt[gs, cs], kt[gs, cs], _NT, preferred_element_type=F32), 0.0)
               for cs in heads]
        o_intra = [jnp.dot(a.astype(BF16), vb[gs, cs], preferred_element_type=F32) for a, cs in zip(a_s, heads)]
        o_inter = [[] for _ in heads]
        for j in range(grp // HG_BLOCK):
            r0 = g * grp + j * HG_BLOCK
            rs = slice(r0, r0 + HG_BLOCK)
            sts = [st_ref[h] for h in range(HG_HEADS)]
            ups = [lax.dot_general(vb[rs, cs], kd[rs, cs], _TN, preferred_element_type=F32) for cs in heads]
            for h, cs in enumerate(heads):
                o_inter[h].append(lax.dot_general(qt[rs, cs], sts[h].astype(BF16), _NT, preferred_element_type=F32))
            for h, cs in enumerate(heads):
                dec = jnp.exp(b[r0 + HG_BLOCK - 1:r0 + HG_BLOCK, cs])
                st_ref[h] = sts[h] * dec + ups[h]
        for h in range(HG_HEADS):
            o_groups[h].append(o_intra[h] + jnp.concatenate(o_inter[h], axis=0))
    o_heads = []
    for h, cs in enumerate(heads):
        o_h = jnp.concatenate(o_groups[h], axis=0)
        zs = hz[:, cs]
        o_heads.append(_rms(o_h) * gh_ref[:, cs] * (zs * _sigmoid(zs)))
    y_ref[...] = jnp.concatenate(o_heads, axis=1).astype(BF16)

    @pl.when(t == pl.num_programs(1) - 1)
    def _():
        for h in range(HG_HEADS):
            s_out_ref[h] = st_ref[h].T


def _hgrn(hg3, hgrn_lb, g_hgrn, s0, tc):
    nb, tl, _ = hg3.shape
    has_s0 = s0 is not None
    st_spec = pl.BlockSpec((None, HG_HEADS, HG_DK, HG_DV), lambda b, t: (b, 0, 0, 0))
    in_specs = [pl.BlockSpec((None, tc, 4 * HG_WIDTH), lambda b, t: (b, t, 0)),
                pl.BlockSpec((2, HG_WIDTH), lambda b, t: (0, 0)),
                pl.BlockSpec((1, HG_WIDTH), lambda b, t: (0, 0))]
    args = [hg3, hgrn_lb, g_hgrn.reshape(1, HG_WIDTH)]
    if has_s0:
        in_specs.append(st_spec)
        args.append(s0)
    return pl.pallas_call(
        functools.partial(_hgrn_kernel, tc=tc, has_s0=has_s0),
        grid=(nb, tl // tc),
        in_specs=in_specs,
        out_specs=[pl.BlockSpec((None, tc, HG_WIDTH), lambda b, t: (b, t, 0)), st_spec],
        out_shape=[jax.ShapeDtypeStruct((nb, tl, HG_WIDTH), BF16),
                   jax.ShapeDtypeStruct((nb, HG_HEADS, HG_DK, HG_DV), F32)],
        scratch_shapes=[pltpu.VMEM((HG_HEADS, HG_DV, HG_DK), F32)],
        compiler_params=_cparams(2),
        name="hgrn",
    )(*args)


def _merge_kernel(a_ref, b_ref, z_ref, x_ref, mod_ref, wa_ref, wb_ref, wo_ref, gpost_ref, gpre_ref,
                  wrh_ref, wrl_ref, br_ref, x1_ref, h2_ref, gate_ref, loc_ref, cnt_ref, *, tm, nseg, sub):
    i = pl.program_id(0)
    seg = tm // nseg

    def mod_rows(r, part):
        if nseg == 1:
            return mod_ref[0, r:r + 1, :]
        segs = range(part * sub // seg, (part + 1) * sub // seg)
        return jnp.concatenate([jnp.broadcast_to(mod_ref[s, r:r + 1, :], (seg, D_MODEL)) for s in segs], axis=0)

    @pl.when(i == 0)
    def _():
        cnt_ref[...] = jnp.zeros_like(cnt_ref)

    earlier = (lax.broadcasted_iota(I32, (sub, sub), 0) < lax.broadcasted_iota(I32, (sub, sub), 1)).astype(BF16)
    lower = (lax.broadcasted_iota(I32, (N_EXPERTS, N_EXPERTS), 1)
             < lax.broadcasted_iota(I32, (N_EXPERTS, N_EXPERTS), 0)).astype(BF16)
    eio = lax.broadcasted_iota(I32, (N_EXPERTS, sub), 0).astype(F32)
    tile_lane = lax.broadcasted_iota(I32, cnt_ref.shape, 1)
    parts = range(tm // sub)
    rss = [slice(part * sub, (part + 1) * sub) for part in parts]
    ms = [(_sigmoid(z_ref[rs, 0:D_MODEL]) * jnp.dot(a_ref[rs, :], wa_ref[...], preferred_element_type=F32)
           + _sigmoid(z_ref[rs, D_MODEL:2 * D_MODEL]) * jnp.dot(b_ref[rs, :], wb_ref[...], preferred_element_type=F32))
          for rs in rss]
    ys = [jnp.dot(m.astype(BF16), wo_ref[...], preferred_element_type=F32) for m in ms]
    x1s = [x_ref[rs, :] + mod_rows(2, part) * (_rms(y) * gpost_ref[...]) for part, rs, y in zip(parts, rss, ys)]
    for rs, x1 in zip(rss, x1s):
        x1_ref[rs, :] = x1
    h2s = [(_rms(x1) * gpre_ref[...]) * (1.0 + mod_rows(4, part)) + mod_rows(3, part) for part, x1 in zip(parts, x1s)]
    h_his = [h2.astype(BF16) for h2 in h2s]
    for rs, h_hi in zip(rss, h_his):
        h2_ref[rs, :] = h_hi

    h_los = [(h2 - h_hi.astype(F32)).astype(BF16) for h2, h_hi in zip(h2s, h_his)]
    lgs = [(jnp.dot(h_hi, wrh_ref[...], preferred_element_type=F32)
            + jnp.dot(h_hi, wrl_ref[...], preferred_element_type=F32)
            + jnp.dot(h_lo, wrh_ref[...], preferred_element_type=F32)) for h_hi, h_lo in zip(h_his, h_los)]
    curs = [lg.T[0:N_EXPERTS, :] + br_ref[...] for lg in lgs]
    vals = [[] for _ in parts]
    sels = [[] for _ in parts]
    for k in range(TOP_K):
        for part in parts:
            cur = curs[part]
            mx = jnp.max(cur, axis=0, keepdims=True)
            ik = jnp.min(jnp.where(cur == mx, eio, float(N_EXPERTS)), axis=0, keepdims=True)
            sel = eio == ik
            vals[part].append(mx)
            sels[part].append(sel)
            curs[part] = jnp.where(sel, -jnp.inf, cur)
    for part, rs in zip(parts, rss):
        es = [jnp.exp(v - vals[part][0]) for v in vals[part]]
        den = es[0] + es[1] + es[2] + es[3]
        for k in range(TOP_K):
            gate_ref[k:k + 1, rs] = es[k] / den

    for part, rs in zip(parts, rss):
        onehot = jnp.zeros((N_EXPERTS, sub), F32)
        for k in range(TOP_K):
            onehot = onehot + sels[part][k].astype(F32)
        n8 = jnp.ceil(jnp.sum(onehot, axis=1, keepdims=True) * 0.125) * 8.0
        base = (jnp.dot(lower, jnp.broadcast_to(n8, onehot.shape).astype(BF16), preferred_element_type=F32)
                + jnp.dot(onehot.astype(BF16), earlier, preferred_element_type=F32))
        for k in range(TOP_K):
            loc_ref[k:k + 1, rs] = jnp.sum(jnp.where(sels[part][k], base, 0.0), axis=0, keepdims=True).astype(I32)
        cnt_ref[...] = jnp.where(tile_lane == i * (tm // sub) + part, n8, cnt_ref[...])


def _merge(a2, b2, z2, x2, ada3, b_off, seq, wa, wb, wo, g_post, g_pre, wr_hi, wr_lo, br, tm, sub):
    n = x2.shape[0]
    if tm >= seq:
        nseg = tm // seq
        assert b_off % nseg == 0
        mod_idx = lambda i: (i + b_off // nseg, 0, 0)
    else:
        nseg = 1
        tpb = seq // tm
        mod_idx = lambda i: (i // tpb + b_off, 0, 0)
    row = lambda i: (i, 0)
    const = lambda i: (0, 0)
    col = lambda i: (0, i)
    return pl.pallas_call(
        functools.partial(_merge_kernel, tm=tm, nseg=nseg, sub=sub),
        grid=(n // tm,),
        in_specs=[pl.BlockSpec((tm, ATT_WIDTH), row),
                  pl.BlockSpec((tm, HG_WIDTH), row),
                  pl.BlockSpec((tm, 2 * D_MODEL), row),
                  pl.BlockSpec((tm, D_MODEL), row),
                  pl.BlockSpec((nseg, 6, D_MODEL), mod_idx),
                  pl.BlockSpec((ATT_WIDTH, D_MODEL), const),
                  pl.BlockSpec((HG_WIDTH, D_MODEL), const),
                  pl.BlockSpec((D_MODEL, D_MODEL), const),
                  pl.BlockSpec((1, D_MODEL), const),
                  pl.BlockSpec((1, D_MODEL), const),
                  pl.BlockSpec((D_MODEL, LANES), const),
                  pl.BlockSpec((D_MODEL, LANES), const),
                  pl.BlockSpec((N_EXPERTS, 1), const)],
        out_specs=[pl.BlockSpec((tm, D_MODEL), row),
                   pl.BlockSpec((tm, D_MODEL), row),
                   pl.BlockSpec((TOP_K, tm), col),
                   pl.BlockSpec((TOP_K, tm), col),
                   pl.BlockSpec((N_EXPERTS, n // sub), const)],
        out_shape=[jax.ShapeDtypeStruct((n, D_MODEL), F32),
                   jax.ShapeDtypeStruct((n, D_MODEL), BF16),
                   jax.ShapeDtypeStruct((TOP_K, n), F32),
                   jax.ShapeDtypeStruct((TOP_K, n), I32),
                   jax.ShapeDtypeStruct((N_EXPERTS, n // sub), F32)],
        compiler_params=_cparams(1),
        name="merge",
    )(a2, b2, z2, x2, ada3, wa, wb, wo, g_post.reshape(1, D_MODEL), g_pre.reshape(1, D_MODEL),
      wr_hi, wr_lo, br.reshape(N_EXPERTS, 1))


def _pow2_pieces(n, max_piece, fn):
    sz = max_piece
    while sz >= SUBLANES:
        off = n & ~(2 * sz - 1)

        @pl.when((n & sz) != 0)
        def _(off=off, sz=sz):
            fn(off, sz)

        sz //= 2


def _start_groups(meta_ref, td, make_copy):
    def body(e, _):
        loc0 = meta_ref[0, e]
        n8 = meta_ref[0, N_EXPERTS + e]
        glob0 = meta_ref[0, 2 * N_EXPERTS + e]
        _pow2_pieces(n8, td, lambda off, sz: make_copy(pl.multiple_of(loc0 + off, SUBLANES),
                                                       pl.multiple_of(glob0 + off, SUBLANES), sz).start())
        return 0

    lax.fori_loop(0, N_EXPERTS, body, 0, unroll=2)


def _wait_groups(meta_ref, n_sorted, make_wait):
    max_piece = 1 << (n_sorted.bit_length() - 1)
    _pow2_pieces(meta_ref[0, 3 * N_EXPERTS], max_piece, lambda off, sz: make_wait(sz).wait())


def _dispatch_kernel(meta_ref, mprev_ref, tail_ref, loc_a_ref, h_a_ref, loc_b_ref, h_b_ref, buf_ref,
                     srt_ref, zero_ref, sem, zsem, *, td, blk, n_a):
    i = pl.program_id(0)
    slot = i % 2
    n_sorted = srt_ref.shape[1]

    @pl.when(i == 0)
    def _():
        zero_ref[...] = jnp.zeros_like(zero_ref)

        def zero_copy(start, off, sz):
            return pltpu.make_async_copy(zero_ref.at[pl.ds(0, sz)],
                                         buf_ref.at[pl.ds(pl.multiple_of(start + off, SUBLANES), sz)], zsem)

        for act in (lambda c: c.start(), lambda c: c.wait()):
            def body(e, _, act=act):
                start = tail_ref[0, e]
                _pow2_pieces(tail_ref[1, e], blk // 2, lambda off, sz: act(zero_copy(start, off, sz)))
                return 0

            lax.fori_loop(0, N_EXPERTS, body, 0)

            def unused(j, _, act=act):
                act(zero_copy(tail_ref[0, N_EXPERTS], j * (blk // 2), blk // 2))
                return 0

            lax.fori_loop(0, tail_ref[1, N_EXPERTS] // (blk // 2), unused, 0)

    def sort_rows(loc_ref, h_ref):
        for j0 in range(0, n_sorted, SORT_CHUNK):
            rio = j0 + lax.broadcasted_iota(I32, (SORT_CHUNK, td), 0)
            hit = rio == loc_ref[0:1, :]
            for k in range(1, TOP_K):
                hit = hit | (rio == loc_ref[k:k + 1, :])
            srt_ref[slot, j0:j0 + SORT_CHUNK, :] = jnp.dot(hit.astype(BF16), h_ref[...],
                                                           preferred_element_type=F32)

    @pl.when(i < n_a)
    def _():
        sort_rows(loc_a_ref, h_a_ref)

    @pl.when(i >= n_a)
    def _():
        sort_rows(loc_b_ref, h_b_ref)

    def make_copy(s):
        return lambda loc0, glob0, sz: pltpu.make_async_copy(
            srt_ref.at[s, pl.ds(loc0, sz)], buf_ref.at[pl.ds(glob0, sz)], sem.at[s])

    _start_groups(meta_ref, td, make_copy(slot))

    @pl.when(i > 0)
    def _():
        _wait_groups(mprev_ref, n_sorted, lambda sz: make_copy(1 - slot)(0, 0, sz))

    @pl.when(i == pl.num_programs(0) - 1)
    def _():
        _wait_groups(meta_ref, n_sorted, lambda sz: make_copy(slot)(0, 0, sz))


def _dispatch(h2_a, loc_a, h2_b, loc_b, meta, tail, n_rows, td, blk):
    n_a, n_b = h2_a.shape[0] // td, h2_b.shape[0] // td
    n_sorted = TOP_K * td + SUBLANES * N_EXPERTS
    tile_a = lambda i: jnp.minimum(i, n_a - 1)
    tile_b = lambda i: jnp.maximum(i - n_a, 0)
    return pl.pallas_call(
        functools.partial(_dispatch_kernel, td=td, blk=blk, n_a=n_a),
        grid=(n_a + n_b,),
        in_specs=[pl.BlockSpec((None, 1, META_LEN), lambda i: (i, 0, 0), memory_space=pltpu.SMEM),
                  pl.BlockSpec((None, 1, META_LEN), lambda i: (jnp.maximum(i - 1, 0), 0, 0),
                               memory_space=pltpu.SMEM),
                  pl.BlockSpec(memory_space=pltpu.SMEM),
                  pl.BlockSpec((TOP_K, td), lambda i: (0, tile_a(i))),
                  pl.BlockSpec((td, D_MODEL), lambda i: (tile_a(i), 0)),
                  pl.BlockSpec((TOP_K, td), lambda i: (0, tile_b(i))),
                  pl.BlockSpec((td, D_MODEL), lambda i: (tile_b(i), 0))],
        out_specs=pl.BlockSpec(memory_space=pl.ANY),
        out_shape=jax.ShapeDtypeStruct((n_rows, D_MODEL), F32),
        scratch_shapes=[pltpu.VMEM((2, n_sorted, D_MODEL), F32), pltpu.VMEM((blk // 2, D_MODEL), F32),
                        pltpu.SemaphoreType.DMA((2,)), pltpu.SemaphoreType.DMA],
        compiler_params=_cparams(1),
        name="dispatch",
    )(meta, meta, tail, loc_a, h2_a, loc_b, h2_b)


def _expert_kernel(be_ref, nu_ref, x_ref, wgu_ref, bgu_ref, wd_ref, bd_ref, y_ref, wgu_bf, wd_bf):
    i = pl.program_id(0)

    @pl.when(i < nu_ref[0])
    def _():
        @pl.when((i == 0) | (be_ref[i] != be_ref[jnp.maximum(i - 1, 0)]))
        def _():
            wgu_bf[...] = wgu_ref[...].astype(BF16)
            wd_bf[...] = wd_ref[...].astype(BF16)

        gu = jnp.dot(x_ref[...].astype(BF16), wgu_bf[...], preferred_element_type=F32) + bgu_ref[...]
        gate = jnp.minimum(gu[:, :D_FF], SWIGLU_LIMIT)
        up = jnp.clip(gu[:, D_FF:], -SWIGLU_LIMIT, SWIGLU_LIMIT)
        act = gate * _sigmoid(SWIGLU_ALPHA * gate) * (up + 1.0)
        y_ref[...] = jnp.dot(act.astype(BF16), wd_bf[...], preferred_element_type=F32) + bd_ref[...]

    @pl.when(i >= nu_ref[0])
    def _():
        y_ref[...] = jnp.zeros_like(y_ref)


def _experts(buf, blk_expert, n_used, wgu, bgu, wd, bd, blk):
    n_rows = buf.shape[0]
    used = lambda i, be, nu: (jnp.minimum(i, nu[0] - 1), 0)
    grid_spec = pltpu.PrefetchScalarGridSpec(
        num_scalar_prefetch=2,
        grid=(n_rows // blk,),
        in_specs=[pl.BlockSpec((blk, D_MODEL), used),
                  pl.BlockSpec((None, D_MODEL, 2 * D_FF), lambda i, be, nu: (be[i], 0, 0)),
                  pl.BlockSpec((None, 1, 2 * D_FF), lambda i, be, nu: (be[i], 0, 0)),
                  pl.BlockSpec((None, D_FF, D_MODEL), lambda i, be, nu: (be[i], 0, 0)),
                  pl.BlockSpec((None, 1, D_MODEL), lambda i, be, nu: (be[i], 0, 0))],
        out_specs=pl.BlockSpec((blk, D_MODEL), lambda i, be, nu: (i, 0)),
        scratch_shapes=[pltpu.VMEM((D_MODEL, 2 * D_FF), BF16), pltpu.VMEM((D_FF, D_MODEL), BF16)],
    )
    return pl.pallas_call(
        _expert_kernel,
        grid_spec=grid_spec,
        out_shape=jax.ShapeDtypeStruct((n_rows, D_MODEL), F32),
        compiler_params=_cparams(1),
        name="experts",
    )(blk_expert, n_used, buf, wgu, bgu.reshape(N_EXPERTS, 1, 2 * D_FF), wd, bd.reshape(N_EXPERTS, 1, D_MODEL))


def _combine_kernel(meta_ref, mnext_ref, loc_ref, gate_ref, x1_ref, mod_ref, gpost_ref, ybuf_ref, o_ref,
                    ys_ref, sem, *, td, nseg):
    i = pl.program_id(0)
    slot = i % 2
    n_sorted = ys_ref.shape[1]

    def make_copy(s):
        return lambda loc0, glob0, sz: pltpu.make_async_copy(
            ybuf_ref.at[pl.ds(glob0, sz)], ys_ref.at[s, pl.ds(loc0, sz)], sem.at[s])

    @pl.when(i == 0)
    def _():
        ys_ref[...] = jnp.zeros_like(ys_ref)
        _start_groups(meta_ref, td, make_copy(0))

    @pl.when(i + 1 < pl.num_programs(0))
    def _():
        _start_groups(mnext_ref, td, make_copy(1 - slot))

    _wait_groups(meta_ref, n_sorted, lambda sz: make_copy(slot)(0, 0, sz))
    mo = None
    for j0 in range(0, n_sorted, SORT_CHUNK):
        jio = j0 + lax.broadcasted_iota(I32, (td, SORT_CHUNK), 1)
        w = jnp.zeros((td, SORT_CHUNK), F32)
        for k in range(TOP_K):
            w = w + jnp.where(jio == loc_ref[:, k:k + 1], gate_ref[:, k:k + 1], 0.0)
        w_hi = w.astype(BF16)
        w_lo = (w - w_hi.astype(F32)).astype(BF16)
        y = ys_ref[slot, j0:j0 + SORT_CHUNK, :].astype(BF16)
        part = jnp.dot(w_hi, y, preferred_element_type=F32) + jnp.dot(w_lo, y, preferred_element_type=F32)
        mo = part if mo is None else mo + part
    if nseg == 1:
        gate2 = mod_ref[0, 5:6, :]
    else:
        gate2 = jnp.concatenate([jnp.broadcast_to(mod_ref[s, 5:6, :], (td // nseg, D_MODEL)) for s in range(nseg)],
                                axis=0)
    o_ref[...] = x1_ref[...] + gate2 * (_rms(mo) * gpost_ref[...])


def _combine(meta, loc_t, gates_t, x1, ada3, b_off, seq, g_post, ybuf, td):
    n = x1.shape[0]
    n_tiles = n // td
    n_sorted = TOP_K * td + SUBLANES * N_EXPERTS
    if td >= seq:
        nseg = td // seq
        assert b_off % nseg == 0
        mod_idx = lambda i: (i + b_off // nseg, 0, 0)
    else:
        nseg = 1
        tpb = seq // td
        mod_idx = lambda i: (i // tpb + b_off, 0, 0)
    return pl.pallas_call(
        functools.partial(_combine_kernel, td=td, nseg=nseg),
        grid=(n_tiles,),
        in_specs=[pl.BlockSpec((None, 1, META_LEN), lambda i: (i, 0, 0), memory_space=pltpu.SMEM),
                  pl.BlockSpec((None, 1, META_LEN), lambda i: (jnp.minimum(i + 1, n_tiles - 1), 0, 0),
                               memory_space=pltpu.SMEM),
                  pl.BlockSpec((td, TOP_K), lambda i: (i, 0)),
                  pl.BlockSpec((td, TOP_K), lambda i: (i, 0)),
                  pl.BlockSpec((td, D_MODEL), lambda i: (i, 0)),
                  pl.BlockSpec((nseg, 6, D_MODEL), mod_idx),
                  pl.BlockSpec((1, D_MODEL), lambda i: (0, 0)),
                  pl.BlockSpec(memory_space=pl.ANY)],
        out_specs=pl.BlockSpec((td, D_MODEL), lambda i: (i, 0)),
        out_shape=jax.ShapeDtypeStruct((n, D_MODEL), F32),
        scratch_shapes=[pltpu.VMEM((2, n_sorted, D_MODEL), F32), pltpu.SemaphoreType.DMA((2,))],
        compiler_params=_cparams(1),
        name="combine",
    )(meta, meta, loc_t, gates_t, x1, ada3, g_post.reshape(1, D_MODEL), ybuf)


def _tile_sizes(nb, seq):
    n = nb * seq
    big = n >= 8192
    tmm = 512 if seq >= 512 else n
    return dict(
        tm=min(seq, 512),
        tmm=tmm,
        tq=min(seq, 512),
        tc=min(seq, 512),
        td=min(tmm, 256),
        blk=512 if big else 128,
    )


def _rope_tables(pos):
    half = HEAD_DIM // 2
    inv_freq = ROPE_THETA ** (-jnp.arange(half, dtype=F32) / half)
    ang = pos.astype(F32)[:, None] * inv_freq[None, :]
    cos, sin = jnp.cos(ang), jnp.sin(ang)
    reps = LANES // HEAD_DIM
    return (jnp.tile(jnp.concatenate([cos, cos], axis=1), (1, reps)),
            jnp.tile(jnp.concatenate([-sin, sin], axis=1), (1, reps)))


def _route_plan(cnt8, n_assign, blk):
    n_tiles = cnt8.shape[1]
    n_blocks = -(-(n_assign + n_tiles * N_EXPERTS * (SUBLANES - 1) + N_EXPERTS * (blk - 1)) // blk)
    tot = jnp.sum(cnt8, axis=1)
    padded = (tot + blk - 1) // blk * blk
    pend = jnp.cumsum(padded)
    pstart = pend - padded
    glob0 = pstart[:, None] + jnp.cumsum(cnt8, axis=1) - cnt8
    loc0 = jnp.cumsum(cnt8, axis=0) - cnt8
    meta = jnp.concatenate([loc0.T, cnt8.T, glob0.T, jnp.sum(cnt8, axis=0)[:, None]],
                           axis=1).reshape(n_tiles, 1, META_LEN)
    blk_row = jnp.arange(n_blocks, dtype=I32) * blk
    blk_expert = jnp.minimum(jnp.sum((pend[None, :] <= blk_row[:, None]).astype(I32), axis=1), N_EXPERTS - 1)
    n_used = (pend[-1:] // blk).astype(I32)
    n_rows = n_blocks * blk
    tail = jnp.stack([jnp.concatenate([pstart + tot, pend[-1:]]),
                      jnp.concatenate([padded - tot, n_rows - pend[-1:]])])
    return meta.astype(I32), tail.astype(I32), blk_expert, n_used, n_rows


def _mixer_and_route(x, ada3, b_off, pos, k_past, v_past, s0, wts):
    nb, seq, _ = x.shape
    n = nb * seq
    ts = _tile_sizes(nb, seq)
    x2 = x.reshape(n, D_MODEL)
    cos_t, sin_t = _rope_tables(pos)
    q, k, v, hg, z = _proj(x2, ada3, b_off, seq, wts['g_pre_mix'], wts['w_in'], cos_t, sin_t, ts['tm'])
    k3 = k.reshape(nb, seq, KV_WIDTH)
    v3 = v.reshape(nb, seq, KV_WIDTH)
    if k_past is None:
        koff = 0
    else:
        rows = k_past.shape[1]
        koff = rows
        k3 = jnp.concatenate([k_past.reshape(nb, rows, KV_WIDTH), k3], axis=1)
        v3 = jnp.concatenate([v_past.reshape(nb, rows, KV_WIDTH), v3], axis=1)
    y_att = _attn(q.reshape(nb, seq, ATT_WIDTH), k3, v3, wts['sinks'], ts['tq'], koff)
    y_hg, s_new = _hgrn(hg.reshape(nb, seq, 4 * HG_WIDTH), wts['hgrn_lb'], wts['g_hgrn'], s0, ts['tc'])
    x1, h2, gates, loc, cnt = _merge(
        y_att.reshape(n, ATT_WIDTH), y_hg.reshape(n, HG_WIDTH), z, x2, ada3, b_off, seq,
        wts['w_br_attn'], wts['w_br_hgrn'], wts['w_out'], wts['g_post_mix'], wts['g_pre_ffn'],
        wts['w_router_hi'], wts['w_router_lo'], wts['b_router'], ts['tmm'], ts['td'])
    win = min(WINDOW, k3.shape[1])
    k_new = k3[:, k3.shape[1] - win:].reshape(nb, win, N_KV_HEADS, HEAD_DIM)
    v_new = v3[:, v3.shape[1] - win:].reshape(nb, win, N_KV_HEADS, HEAD_DIM)
    route = dict(x1=x1, h2=h2, gates=gates, loc=loc, cnt=cnt.astype(I32), td=ts['td'], blk=ts['blk'],
                 b_off=b_off, shape=(nb, seq))
    return route, k_new, v_new, s_new


def _moe(routes, ada3, wts):
    ra, rb = routes
    assert ra['td'] == rb['td']
    blk = max(r['blk'] for r in routes)
    n_assign = sum(r['h2'].shape[0] for r in routes) * TOP_K
    meta, tail, blk_expert, n_used, n_rows = _route_plan(
        jnp.concatenate([r['cnt'] for r in routes], axis=1), n_assign, blk)
    metas, t0 = [], 0
    for r in routes:
        metas.append(meta[t0:t0 + r['cnt'].shape[1]])
        t0 += r['cnt'].shape[1]
    buf = _dispatch(ra['h2'], ra['loc'], rb['h2'], rb['loc'], meta, tail, n_rows, ra['td'], blk)
    ybuf = _experts(buf, blk_expert, n_used, wts['w_gate_up'], wts['b_gate_up'], wts['w_down'], wts['b_down'], blk)
    outs = []
    for r, m in zip(routes, metas):
        nb, seq = r['shape']
        out = _combine(m, r['loc'].T, r['gates'].T, r['x1'], ada3, r['b_off'], seq, wts['g_post_ffn'], ybuf, r['td'])
        outs.append(out.reshape(nb, seq, D_MODEL))
    return outs


def kernel(x_prompt, x_sample, cache_k, cache_v, state_hgrn, c_prompt, c_sample, w_ada, b_ada, g_pre_mix, g_post_mix, g_pre_ffn, g_post_ffn, w_in, attn_sinks, hgrn_lb, g_hgrn, w_br_attn, w_br_hgrn, w_out, w_router, b_router, w_gate_up, b_gate_up, w_down, b_down):
    n_bp = x_prompt.shape[0]
    wr = jnp.pad(w_router[0], ((0, 0), (0, LANES - N_EXPERTS)))
    wr_hi = wr.astype(BF16)
    wr_lo = (wr - wr_hi.astype(F32)).astype(BF16)
    wts = dict(
        g_pre_mix=g_pre_mix[0], g_post_mix=g_post_mix[0], g_pre_ffn=g_pre_ffn[0], g_post_ffn=g_post_ffn[0],
        w_in=w_in[0].astype(BF16), sinks=attn_sinks[0], hgrn_lb=hgrn_lb, g_hgrn=g_hgrn[0],
        w_br_attn=w_br_attn[0].astype(BF16), w_br_hgrn=w_br_hgrn[0].astype(BF16), w_out=w_out[0].astype(BF16),
        w_router_hi=wr_hi, w_router_lo=wr_lo, b_router=b_router[0],
        w_gate_up=w_gate_up[0], b_gate_up=b_gate_up[0], w_down=w_down[0], b_down=b_down[0])
    ada = _ada(jnp.concatenate([c_prompt, c_sample], axis=0), w_ada[0], b_ada[0])
    ada3 = ada.reshape(ada.shape[0], 6, D_MODEL)
    pos_p = jnp.arange(x_prompt.shape[1])
    pos_s = PAST_LEN + jnp.arange(x_sample.shape[1])
    route_s, ks, vs, ss = _mixer_and_route(x_sample, ada3, n_bp, pos_s, cache_k[0], cache_v[0], state_hgrn[0], wts)
    route_p, kp, vp, sp = _mixer_and_route(x_prompt, ada3, 0, pos_p, None, None, None, wts)
    ys, yp = _moe([route_s, route_p], ada3, wts)
    return (yp, ys, kp[None], vp[None], sp[None], ks[None], vs[None], ss[None])
```
